```python
import jax, jax.numpy as jnp
from jax import lax
import numpy as np

D_MODEL = 4096
BATCH = 4
SEQ = 2048
DEPTH = 1
DEC_BATCH = 128
DEC_SEQ = 1
PAST_LEN = 16384
PAGE_SIZE = 128

N_META = 16
H_A = 32
Q_RANK = 1024
KV_RANK = 512
NOPE_DIM = 128
ROPE_DIM = 64
V_DIM = 128
ROPE_THETA = 10000.0
ATTN_SCALE = (NOPE_DIM + ROPE_DIM) ** -0.5
Q_BLOCK = 128
H_M = 8
DQK = D_MODEL // 2 // H_M
DV = D_MODEL // H_M
CHUNK = 64
F_BIAS = 3.0
N_EXPERTS = 32
TOP_K = 4
D_FF = D_MODEL
SWIGLU_LIMIT = 7.0
SWIGLU_ALPHA = 1.702
ALPHA = (2 * DEPTH) ** 0.25
BETA = (8 * DEPTH) ** -0.25
EPS = 1e-5
IN_SIZES = (Q_RANK, KV_RANK, ROPE_DIM, H_M * DQK, H_M * DQK, H_M * DV, H_M * DV, H_M, H_M, D_MODEL, D_MODEL)
IN_COLS = sum(IN_SIZES)

kernel_name = "mla_mlstm_gated_moe_deepnorm_step"


def layernorm(x, g, b):
    xf = x.astype(jnp.float32)
    mu = jnp.mean(xf, axis=-1, keepdims=True)
    var = jnp.mean(jnp.square(xf - mu), axis=-1, keepdims=True)
    return ((xf - mu) * lax.rsqrt(var + EPS) * g + b).astype(x.dtype)


def rmsnorm(x, g):
    xf = x.astype(jnp.float32)
    return (xf * lax.rsqrt(jnp.mean(xf * xf, axis=-1, keepdims=True) + EPS) * g).astype(x.dtype)


def rope(x, pos):
    inv = 1.0 / (ROPE_THETA ** (jnp.arange(0, ROPE_DIM, 2, dtype=jnp.float32) / ROPE_DIM))
    ang = pos.astype(jnp.float32)[:, None] * inv
    ang = ang.reshape(ang.shape[:1] + (1,) * (x.ndim - 3) + ang.shape[1:])
    cos, sin = jnp.cos(ang), jnp.sin(ang)
    x1, x2 = jnp.split(x.astype(jnp.float32), 2, axis=-1)
    return jnp.concatenate([x1 * cos - x2 * sin, x2 * cos + x1 * sin], axis=-1).astype(x.dtype)


def split_in(z):
    return jnp.split(z, np.cumsum(IN_SIZES)[:-1].tolist(), axis=-1)


def mla_queries(cq, pos, q_norm_g, w_uq):
    q = rmsnorm(cq, q_norm_g) @ w_uq
    q = q.reshape(q.shape[:-1] + (H_A, NOPE_DIM + ROPE_DIM))
    return q[..., :NOPE_DIM], rope(q[..., NOPE_DIM:], pos)


def mla_prompt_attend(q_nope, q_rope, ckv, kr, w_uk, w_uv):
    B, L = ckv.shape[:2]
    k_nope = jnp.einsum('blr,rhe->blhe', ckv, w_uk)
    v = jnp.einsum('blr,rhe->blhe', ckv, w_uv)
    nb = -(-L // Q_BLOCK)
    lp = nb * Q_BLOCK

    def blocks(t):
        t = jnp.pad(t, ((0, 0), (0, lp - L), (0, 0), (0, 0)))
        return jnp.moveaxis(t.reshape((B, nb, Q_BLOCK) + t.shape[2:]), 1, 0)

    q_pos = jnp.arange(lp).reshape(nb, Q_BLOCK)
    k_pos = jnp.arange(L)

    def attend(args):
        qn, qr, qp = args
        s = (jnp.einsum('bqhe,bkhe->bhqk', qn, k_nope)
             + jnp.einsum('bqhe,bke->bhqk', qr, kr)).astype(jnp.float32) * ATTN_SCALE
        s = jnp.where(k_pos[None, :] <= qp[:, None], s, -jnp.inf)
        p = jax.nn.softmax(s, axis=-1).astype(v.dtype)
        return jnp.einsum('bhqk,bkhe->bqhe', p, v)

    o = lax.map(attend, (blocks(q_nope), blocks(q_rope), q_pos))
    return jnp.moveaxis(o, 0, 1).reshape(B, lp, H_A * V_DIM)[:, :L]


def mla_sample_attend(q_nope, q_rope, ckv, kr, cache_kv, cache_kr, page_table, w_uk, w_uv):
    DB, S = ckv.shape[:2]
    q_lat = jnp.einsum('bshe,rhe->bshr', q_nope, w_uk)

    def page_step(carry, pages):
        m, l, acc = carry
        kv_blk = cache_kv[pages]
        kr_blk = cache_kr[pages]
        s = (jnp.einsum('bshr,bpr->bshp', q_lat, kv_blk)
             + jnp.einsum('bshe,bpe->bshp', q_rope, kr_blk)).astype(jnp.float32) * ATTN_SCALE
        m_new = jnp.maximum(m, jnp.max(s, axis=-1))
        p = jnp.exp(s - m_new[..., None])
        corr = jnp.exp(m - m_new)
        acc = acc * corr[..., None] + jnp.einsum('bshp,bpr->bshr', p, kv_blk.astype(jnp.float32))
        return (m_new, l * corr + jnp.sum(p, axis=-1), acc), None

    init = (jnp.full((DB, S, H_A), -jnp.inf, jnp.float32),
            jnp.zeros((DB, S, H_A), jnp.float32),
            jnp.zeros((DB, S, H_A, KV_RANK), jnp.float32))
    (m, l, acc), _ = lax.scan(page_step, init, page_table.T)
    s = (jnp.einsum('bshr,btr->bsht', q_lat, ckv)
         + jnp.einsum('bshe,bte->bsht', q_rope, kr)).astype(jnp.float32) * ATTN_SCALE
    s = jnp.where(jnp.tril(jnp.ones((S, S), bool))[:, None, :], s, -jnp.inf)
    m_new = jnp.maximum(m, jnp.max(s, axis=-1))
    p = jnp.exp(s - m_new[..., None])
    corr = jnp.exp(m - m_new)
    l = l * corr + jnp.sum(p, axis=-1)
    acc = acc * corr[..., None] + jnp.einsum('bsht,btr->bshr', p, ckv.astype(jnp.float32))
    o = jnp.einsum('bshr,rhe->bshe', acc / l[..., None], w_uv.astype(jnp.float32))
    return o.reshape(DB, S, H_A * V_DIM).astype(ckv.dtype)


def mlstm_heads(mq, mk, mv, mi, mf, b_if):
    B, L = mq.shape[:2]
    q = mq.reshape(B, L, H_M, DQK).transpose(0, 2, 1, 3) * (DQK ** -0.5)
    k = mk.reshape(B, L, H_M, DQK).transpose(0, 2, 1, 3)
    v = mv.reshape(B, L, H_M, DV).transpose(0, 2, 1, 3)
    log_i = (mi.astype(jnp.float32) + b_if[:H_M]).transpose(0, 2, 1)
    log_f = jax.nn.log_sigmoid(mf.astype(jnp.float32) + b_if[H_M:]).transpose(0, 2, 1)
    return q, k, v, log_i, log_f


def mlstm_chunk(carry, blk):
    C, n, m = carry
    q, k, v, log_i, log_f = blk
    L = q.shape[2]
    F = jnp.cumsum(log_f, axis=-1)
    D = F[..., :, None] - F[..., None, :] + log_i[..., None, :]
    D = jnp.where(jnp.tril(jnp.ones((L, L), bool)), D, -jnp.inf)
    b = F + m[..., None]
    m_row = jnp.maximum(b, jnp.max(D, axis=-1))
    W = jnp.einsum('bhte,bhse->bhts', q, k) * jnp.exp(D - m_row[..., None])
    bw = jnp.exp(b - m_row)
    num = bw[..., None] * jnp.einsum('bhve,bhte->bhtv', C, q) + jnp.einsum('bhts,bhsv->bhtv', W, v)
    den = bw * jnp.einsum('bhe,bhte->bht', n, q) + jnp.sum(W, axis=-1)
    h = num / jnp.maximum(jnp.abs(den), jnp.exp(-m_row))[..., None]
    m_new = m_row[..., -1]
    w_s = jnp.exp(F[..., -1:] - F + log_i - m_new[..., None])
    decay = jnp.exp(F[..., -1] + m - m_new)
    C_new = decay[..., None, None] * C + jnp.einsum('bhsv,bhse->bhve', v * w_s[..., None], k)
    n_new = decay[..., None] * n + jnp.einsum('bhs,bhse->bhe', w_s, k)
    return (C_new, n_new, m_new), h


def mlstm_prompt(q, k, v, log_i, log_f):
    B = q.shape[0]
    init = (jnp.zeros((B, H_M, DV, DQK), jnp.float32),
            jnp.zeros((B, H_M, DQK), jnp.float32),
            jnp.zeros((B, H_M), jnp.float32))
    carry, h_meta = mlstm_chunk(init, (q[:, :, :N_META], k[:, :, :N_META], v[:, :, :N_META],
                                       log_i[:, :, :N_META], log_f[:, :, :N_META]))
    nc = (q.shape[2] - N_META) // CHUNK

    def chunks(t):
        r = t[:, :, N_META:]
        return jnp.moveaxis(r.reshape(r.shape[:2] + (nc, CHUNK) + r.shape[3:]), 2, 0)

    carry, h_rest = lax.scan(mlstm_chunk, carry, (chunks(q), chunks(k), chunks(v), chunks(log_i), chunks(log_f)))
    h_rest = jnp.moveaxis(h_rest, 0, 2).reshape(B, H_M, nc * CHUNK, DV)
    return jnp.concatenate([h_meta, h_rest], axis=2), carry


def mlstm_output(h, mo, m_norm_g):
    B, H, L, _ = h.shape
    hn = h * lax.rsqrt(jnp.mean(h * h, axis=-1, keepdims=True) + EPS)
    hn = hn.transpose(0, 2, 1, 3).reshape(B, L, H * DV) * m_norm_g
    return (hn * jax.nn.sigmoid(mo.astype(jnp.float32))).astype(mo.dtype)


def merge_branches(o_a, o_m, ga, gm, w_br_a, w_br_m, w_out):
    merged = jax.nn.sigmoid(ga) * (o_a @ w_br_a) + jax.nn.sigmoid(gm) * (o_m @ w_br_m)
    return merged @ w_out


def moe(h, w_router, b_router, w_gate, b_gate, w_up, b_up, w_down, b_down):
    shape = h.shape
    x = h.reshape(-1, shape[-1])
    logits = (x @ w_router + b_router).astype(jnp.float32)
    top_v, top_i = lax.top_k(logits, TOP_K)
    probs = jax.nn.softmax(top_v, axis=-1)
    combine = jnp.sum(jax.nn.one_hot(top_i, N_EXPERTS, dtype=jnp.float32) * probs[..., None], axis=1)
    y = jnp.zeros(x.shape, jnp.float32)
    for e in range(N_EXPERTS):
        g = jnp.minimum(x @ w_gate[e] + b_gate[e], SWIGLU_LIMIT)
        u = jnp.clip(x @ w_up[e] + b_up[e], -SWIGLU_LIMIT, SWIGLU_LIMIT)
        act = g * jax.nn.sigmoid(SWIGLU_ALPHA * g) * (u + 1.0)
        y = y + combine[:, e:e + 1] * (act @ w_down[e] + b_down[e])
    return y.astype(h.dtype).reshape(shape)


def setup_inputs(seed: int = 0) -> dict:
    key = jax.random.key(seed)
    k = jax.random.split(key, 40)
    f32 = jnp.float32

    def nrm(i, shape, scale):
        return scale * jax.random.normal(k[i], shape, f32)

    n_pages = PAST_LEN // PAGE_SIZE
    n_used = DEC_BATCH * n_pages
    n_pool = n_used + max(1, n_used // 4)
    page_table = jax.random.permutation(k[0], n_pool)[:n_used].reshape(DEC_BATCH, n_pages).astype(jnp.int32)
    b_if = jnp.concatenate([nrm(1, (DEPTH, H_M), 0.1), F_BIAS + nrm(2, (DEPTH, H_M), 0.1)], axis=-1)
    return {
        "x_prompt": nrm(3, (BATCH, SEQ, D_MODEL), 1.0),
        "x_sample": nrm(4, (DEC_BATCH, DEC_SEQ, D_MODEL), 1.0),
        "cache_kv": nrm(5, (DEPTH, n_pool, PAGE_SIZE, KV_RANK), 1.0),
        "cache_kr": nrm(6, (DEPTH, n_pool, PAGE_SIZE, ROPE_DIM), 1.0),
        "state_C": nrm(7, (DEPTH, DEC_BATCH, H_M, DV, DQK), DQK ** -0.5),
        "state_n": nrm(8, (DEPTH, DEC_BATCH, H_M, DQK), DQK ** -0.5),
        "state_m": jax.random.uniform(k[9], (DEPTH, DEC_BATCH, H_M), f32, 0.0, 4.0),
        "page_table": page_table,
        "meta": nrm(10, (N_META, D_MODEL), 1.0),
        "ln_in_g": 1.0 + nrm(11, (D_MODEL,), 0.01),
        "ln_in_b": nrm(12, (D_MODEL,), 0.01),
        "w_in": nrm(13, (DEPTH, D_MODEL, IN_COLS), D_MODEL ** -0.5),
        "b_if": b_if,
        "q_norm_g": 1.0 + nrm(14, (DEPTH, Q_RANK), 0.01),
        "w_uq": nrm(15, (DEPTH, Q_RANK, H_A * (NOPE_DIM + ROPE_DIM)), Q_RANK ** -0.5),
        "kv_norm_g": 1.0 + nrm(16, (DEPTH, KV_RANK), 0.01),
        "w_uk": nrm(17, (DEPTH, KV_RANK, H_A, NOPE_DIM), KV_RANK ** -0.5),
        "w_uv": nrm(18, (DEPTH, KV_RANK, H_A, V_DIM), KV_RANK ** -0.5),
        "m_norm_g": 1.0 + nrm(19, (DEPTH, H_M * DV), 0.01),
        "w_br_a": nrm(20, (DEPTH, H_A * V_DIM, D_MODEL), BETA * (H_A * V_DIM) ** -0.5),
        "w_br_m": nrm(21, (DEPTH, H_M * DV, D_MODEL), BETA * (H_M * DV) ** -0.5),
        "w_out": nrm(22, (DEPTH, D_MODEL, D_MODEL), BETA * D_MODEL ** -0.5),
        "ln1_g": 1.0 + nrm(23, (DEPTH, D_MODEL), 0.01),
        "ln1_b": nrm(24, (DEPTH, D_MODEL), 0.01),
        "w_router": nrm(25, (DEPTH, D_MODEL, N_EXPERTS), D_MODEL ** -0.5),
        "b_router": nrm(26, (DEPTH, N_EXPERTS), 0.01),
        "w_gate": nrm(27, (DEPTH, N_EXPERTS, D_MODEL, D_FF), D_MODEL ** -0.5),
        "b_gate": nrm(28, (DEPTH, N_EXPERTS, D_FF), 0.01),
        "w_up": nrm(29, (DEPTH, N_EXPERTS, D_MODEL, D_FF), D_MODEL ** -0.5),
        "b_up": nrm(30, (DEPTH, N_EXPERTS, D_FF), 0.01),
        "w_down": nrm(31, (DEPTH, N_EXPERTS, D_FF, D_MODEL), BETA * D_FF ** -0.5),
        "b_down": nrm(32, (DEPTH, N_EXPERTS, D_MODEL), 0.01),
        "ln2_g": 1.0 + nrm(33, (DEPTH, D_MODEL), 0.01),
        "ln2_b": nrm(34, (DEPTH, D_MODEL), 0.01),
    }


def reference(x_prompt, x_sample, cache_kv, cache_kr, state_C, state_n, state_m, page_table,
              meta, ln_in_g, ln_in_b, w_in, b_if, q_norm_g, w_uq, kv_norm_g, w_uk, w_uv, m_norm_g,
              w_br_a, w_br_m, w_out, ln1_g, ln1_b, w_router, b_router, w_gate, b_gate, w_up, b_up,
              w_down, b_down, ln2_g, ln2_b):
    B = x_prompt.shape[0]
    meta_b = jnp.broadcast_to(meta.astype(x_prompt.dtype)[None], (B, N_META, meta.shape[-1]))
    hp = layernorm(jnp.concatenate([meta_b, x_prompt], axis=1), ln_in_g, ln_in_b)
    hs = layernorm(x_sample, ln_in_g, ln_in_b)
    pos_p = jnp.arange(hp.shape[1])
    pos_s = page_table.shape[1] * PAGE_SIZE + jnp.arange(hs.shape[1])
    kv_p, kr_p, C_p, n_p, m_p = [], [], [], [], []
    kv_s, kr_s, C_s, n_s, m_s = [], [], [], [], []
    for l in range(DEPTH):
        cq, ckv, kr_raw, mq, mk, mv, mo, mi, mf, ga, gm = split_in(hp @ w_in[l])
        q_nope, q_rope = mla_queries(cq, pos_p, q_norm_g[l], w_uq[l])
        ckv_n = rmsnorm(ckv, kv_norm_g[l])
        kr = rope(kr_raw, pos_p)
        o_a = mla_prompt_attend(q_nope, q_rope, ckv_n, kr, w_uk[l], w_uv[l])
        h_cell, (Cf, nf, mf_state) = mlstm_prompt(*mlstm_heads(mq, mk, mv, mi, mf, b_if[l]))
        o_m = mlstm_output(h_cell, mo, m_norm_g[l])
        mix = merge_branches(o_a, o_m, ga, gm, w_br_a[l], w_br_m[l], w_out[l])
        hp = layernorm(ALPHA * hp + mix, ln1_g[l], ln1_b[l])
        hp = layernorm(ALPHA * hp + moe(hp, w_router[l], b_router[l], w_gate[l], b_gate[l], w_up[l], b_up[l],
                                        w_down[l], b_down[l]), ln2_g[l], ln2_b[l])
        kv_p.append(ckv_n)
        kr_p.append(kr)
        C_p.append(Cf)
        n_p.append(nf)
        m_p.append(mf_state)
        cq, ckv, kr_raw, mq, mk, mv, mo, mi, mf, ga, gm = split_in(hs @ w_in[l])
        q_nope, q_rope = mla_queries(cq, pos_s, q_norm_g[l], w_uq[l])
        ckv_n = rmsnorm(ckv, kv_norm_g[l])
        kr = rope(kr_raw, pos_s)
        o_a = mla_sample_attend(q_nope, q_rope, ckv_n, kr, cache_kv[l], cache_kr[l], page_table, w_uk[l], w_uv[l])
        carry0 = (state_C[l].astype(jnp.float32), state_n[l].astype(jnp.float32), state_m[l].astype(jnp.float32))
        (Cf, nf, mf_state), h_cell = mlstm_chunk(carry0, mlstm_heads(mq, mk, mv, mi, mf, b_if[l]))
        o_m = mlstm_output(h_cell, mo, m_norm_g[l])
        mix = merge_branches(o_a, o_m, ga, gm, w_br_a[l], w_br_m[l], w_out[l])
        hs = layernorm(ALPHA * hs + mix, ln1_g[l], ln1_b[l])
        hs = layernorm(ALPHA * hs + moe(hs, w_router[l], b_router[l], w_gate[l], b_gate[l], w_up[l], b_up[l],
                                        w_down[l], b_down[l]), ln2_g[l], ln2_b[l])
        kv_s.append(ckv_n)
        kr_s.append(kr)
        C_s.append(Cf)
        n_s.append(nf)
        m_s.append(mf_state)
    y_prompt = hp[:, N_META:]
    y_sample = hs
    return (y_prompt, y_sample,
            jnp.stack(kv_p), jnp.stack(kr_p), jnp.stack(C_p), jnp.stack(n_p), jnp.stack(m_p),
            jnp.stack(kv_s), jnp.stack(kr_s), jnp.stack(C_s), jnp.stack(n_s), jnp.stack(m_s))
```

```python
import functools

import jax
import jax.numpy as jnp
import numpy as np
from jax import lax
from jax.experimental import pallas as pl
from jax.experimental.pallas import tpu as pltpu

F32 = jnp.float32
BF16 = jnp.bfloat16

N_META = 16
H_A = 32
NOPE_DIM = 128
ROPE_DIM = 64
V_DIM = 128
ROPE_THETA = 10000.0
ATTN_SCALE = (NOPE_DIM + ROPE_DIM) ** -0.5
H_M = 8
CHUNK = 64
TOP_K = 4
SWIGLU_LIMIT = 7.0
SWIGLU_ALPHA = 1.702
DEPTH = 1
ALPHA = (2 * DEPTH) ** 0.25
EPS = 1e-5

LANES = 128
VMEM_LIMIT = 52 * 1024 * 1024
ROW_ALIGN = 256
MOE_SUPER = 1024
MOE_SUB = 256
MOE_NSUB = MOE_SUPER // MOE_SUB
PAGES_PER_STEP = 8


def _cp(n_axes):
    return pltpu.CompilerParams(dimension_semantics=("arbitrary",) * n_axes, vmem_limit_bytes=VMEM_LIMIT)


def _tile(n, prefs):
    for t in prefs:
        if n % t == 0:
            return t
    return n


def _dot(a, b):
    return jnp.dot(a, b, preferred_element_type=F32)


def _dot_nt(a, b):
    return lax.dot_general(a, b, (((1,), (1,)), ((), ())), preferred_element_type=F32)


def _dot_tn(a, b):
    return lax.dot_general(a, b, (((0,), (0,)), ((), ())), preferred_element_type=F32)


def _log_sigmoid(x):
    return jnp.minimum(x, 0.0) - jnp.log(1.0 + jnp.exp(-jnp.abs(x)))


def _sigmoid(x):
    return 1.0 / (1.0 + jnp.exp(-x))


def _layernorm(x, g, b):
    mu = jnp.mean(x, axis=-1, keepdims=True)
    xc = x - mu
    var = jnp.mean(xc * xc, axis=-1, keepdims=True)
    return xc * lax.rsqrt(var + EPS) * g + b


def _rmsnorm(x, g):
    return x * lax.rsqrt(jnp.mean(x * x, axis=-1, keepdims=True) + EPS) * g


def _ln_in_kernel(x_ref, g_ref, b_ref, o32_ref, o16_ref):
    y = _layernorm(x_ref[...], g_ref[...], b_ref[...])
    o32_ref[...] = y
    o16_ref[...] = y.astype(BF16)


def _ln_in(x, g, b):
    R, D = x.shape
    tm = _tile(R, (128, 64, 32, 16, 8))
    row = pl.BlockSpec((tm, D), lambda i: (i, 0))
    vec = pl.BlockSpec((1, D), lambda i: (0, 0))
    return pl.pallas_call(
        _ln_in_kernel, grid=(R // tm,), in_specs=[row, vec, vec], out_specs=[row, row],
        out_shape=[jax.ShapeDtypeStruct((R, D), F32), jax.ShapeDtypeStruct((R, D), BF16)],
        compiler_params=_cp(1), name="ln_in")(x, g.reshape(1, D), b.reshape(1, D))


def _mm_kernel(a_ref, b_ref, o_ref):
    o_ref[...] = _dot(a_ref[...], b_ref[...]).astype(o_ref.dtype)


def _matmul(a, b, out_dtype, m_rows=None, name="matmul"):
    K, N = b.shape
    M = a.shape[0] if m_rows is None else m_rows
    tm = _tile(M, (384, 640, 256, 320, 128, 64, 32, 16))
    tn = _tile(N, (1024, 512, 256, 128))
    return pl.pallas_call(
        _mm_kernel, grid=(N // tn, M // tm),
        in_specs=[pl.BlockSpec((tm, K), lambda j, i: (i, 0)), pl.BlockSpec((K, tn), lambda j, i: (0, j))],
        out_specs=pl.BlockSpec((tm, tn), lambda j, i: (i, j)),
        out_shape=jax.ShapeDtypeStruct((M, N), out_dtype),
        compiler_params=_cp(2), name=name)(a, b)


def _rope128(a, cos, sin):
    return a * cos + pltpu.roll(a, ROPE_DIM, 1) * sin


def _post_kernel(cq_ref, ckv_ref, kr_ref, cos_ref, sin_ref, qg_ref, kvg_ref,
                 cqn_ref, kv32_ref, kv16_ref, kr32_ref, kr16_ref):
    cqn_ref[...] = _rmsnorm(cq_ref[...], qg_ref[...]).astype(BF16)
    kvn = _rmsnorm(ckv_ref[...], kvg_ref[...])
    kv32_ref[...] = kvn
    kv16_ref[...] = kvn.astype(BF16)
    r = _rope128(kr_ref[...], cos_ref[...], sin_ref[...])[:, :ROPE_DIM]
    kr32_ref[...] = r
    kr16_ref[...] = r.astype(BF16)


def _post(z, cos, sin, q_norm_g, kv_norm_g, off):
    R = z.shape[0]
    QR, KR = q_norm_g.shape[0], kv_norm_g.shape[0]
    tm = _tile(R, (256, 128, 64, 32, 16))
    return pl.pallas_call(
        _post_kernel, grid=(R // tm,),
        in_specs=[pl.BlockSpec((tm, QR), lambda i: (i, off["cq"] // QR)),
                  pl.BlockSpec((tm, KR), lambda i: (i, off["ckv"] // KR)),
                  pl.BlockSpec((tm, LANES), lambda i: (i, off["kr"] // LANES)),
                  pl.BlockSpec((tm, LANES), lambda i: (i, 0)),
                  pl.BlockSpec((tm, LANES), lambda i: (i, 0)),
                  pl.BlockSpec((1, QR), lambda i: (0, 0)),
                  pl.BlockSpec((1, KR), lambda i: (0, 0))],
        out_specs=[pl.BlockSpec((tm, QR), lambda i: (i, 0)),
                   pl.BlockSpec((tm, KR), lambda i: (i, 0)),
                   pl.BlockSpec((tm, KR), lambda i: (i, 0)),
                   pl.BlockSpec((tm, ROPE_DIM), lambda i: (i, 0)),
                   pl.BlockSpec((tm, ROPE_DIM), lambda i: (i, 0))],
        out_shape=[jax.ShapeDtypeStruct((R, QR), BF16), jax.ShapeDtypeStruct((R, KR), F32),
                   jax.ShapeDtypeStruct((R, KR), BF16), jax.ShapeDtypeStruct((R, ROPE_DIM), F32),
                   jax.ShapeDtypeStruct((R, ROPE_DIM), BF16)],
        compiler_params=_cp(1), name="latent_norm_rope")(
            z, z, z, cos, sin, q_norm_g.reshape(1, QR), kv_norm_g.reshape(1, KR))


def _q_kernel(cqn_ref, w_ref, cos_ref, sin_ref, q_ref):
    acc = _dot(cqn_ref[...], w_ref[0])
    q_ref[0, :, 0:NOPE_DIM] = acc[:, :NOPE_DIM].astype(BF16)
    r = _rope128(acc[:, NOPE_DIM:], cos_ref[...], sin_ref[...])
    q_ref[0, :, NOPE_DIM:NOPE_DIM + ROPE_DIM] = r[:, :ROPE_DIM].astype(BF16)


def _queries(cqn, w_uq3, cos, sin):
    R, QR = cqn.shape
    H = w_uq3.shape[0]
    tm = _tile(R, (768, 512, 256, 128, 64, 32, 16))
    return pl.pallas_call(
        _q_kernel, grid=(R // tm, H),
        in_specs=[pl.BlockSpec((tm, QR), lambda i, h: (i, 0)),
                  pl.BlockSpec((1, QR, NOPE_DIM + 2 * ROPE_DIM), lambda i, h: (h, 0, 0)),
                  pl.BlockSpec((tm, LANES), lambda i, h: (i, 0)),
                  pl.BlockSpec((tm, LANES), lambda i, h: (i, 0))],
        out_specs=pl.BlockSpec((1, tm, NOPE_DIM + ROPE_DIM), lambda i, h: (h, i, 0)),
        out_shape=jax.ShapeDtypeStruct((H, R, NOPE_DIM + ROPE_DIM), BF16),
        compiler_params=_cp(2), name="mla_queries")(cqn, w_uq3, cos, sin)


def _kvx_kernel(kv16_ref, kr16_ref, wuk_ref, wuv_ref, k_ref, v_ref):
    kv = kv16_ref[...]
    k_ref[0, :, 0:NOPE_DIM] = _dot(kv, wuk_ref[0]).astype(BF16)
    k_ref[0, :, NOPE_DIM:NOPE_DIM + ROPE_DIM] = kr16_ref[...]
    v_ref[0] = _dot(kv, wuv_ref[0]).astype(BF16)


def _kv_expand(kv16, kr16, wuk_t, wuv_t):
    R, KR = kv16.shape
    H = wuk_t.shape[0]
    tm = _tile(R, (768, 512, 256, 128, 64, 32, 16))
    return pl.pallas_call(
        _kvx_kernel, grid=(R // tm, H),
        in_specs=[pl.BlockSpec((tm, KR), lambda i, h: (i, 0)),
                  pl.BlockSpec((tm, ROPE_DIM), lambda i, h: (i, 0)),
                  pl.BlockSpec((1, KR, NOPE_DIM), lambda i, h: (h, 0, 0)),
                  pl.BlockSpec((1, KR, V_DIM), lambda i, h: (h, 0, 0))],
        out_specs=[pl.BlockSpec((1, tm, NOPE_DIM + ROPE_DIM), lambda i, h: (h, i, 0)),
                   pl.BlockSpec((1, tm, V_DIM), lambda i, h: (h, i, 0))],
        out_shape=[jax.ShapeDtypeStruct((H, R, NOPE_DIM + ROPE_DIM), BF16),
                   jax.ShapeDtypeStruct((H, R, V_DIM), BF16)],
        compiler_params=_cp(2), name="mla_kv_expand")(kv16, kr16, wuk_t, wuv_t)


def _attn_kernel(q_ref, k_ref, v_ref, km_ref, vm_ref, o_ref, *, tq, tk):
    qi = pl.program_id(2)
    q = q_ref[0]
    s0 = _dot_nt(q, km_ref[0]) * ATTN_SCALE
    m = jnp.max(s0, axis=-1, keepdims=True)
    p0 = jnp.exp(s0 - m)
    l = jnp.sum(p0, axis=-1, keepdims=True)
    acc = _dot(p0.astype(BF16), vm_ref[0])
    row = qi * tq + lax.broadcasted_iota(jnp.int32, (tq, tk), 0)
    col = lax.broadcasted_iota(jnp.int32, (tq, tk), 1)

    def body(j, carry):
        m, l, acc = carry
        start = pl.multiple_of(j * tk, tk)
        k = k_ref[0, pl.ds(start, tk), :]
        v = v_ref[0, pl.ds(start, tk), :]
        s = _dot_nt(q, k) * ATTN_SCALE
        s = jnp.where(col + j * tk <= row, s, -jnp.inf)
        m_new = jnp.maximum(m, jnp.max(s, axis=-1, keepdims=True))
        p = jnp.exp(s - m_new)
        corr = jnp.exp(m - m_new)
        l = l * corr + jnp.sum(p, axis=-1, keepdims=True)
        acc = acc * corr + _dot(p.astype(BF16), v)
        return m_new, l, acc

    m, l, acc = lax.fori_loop(0, (qi + 1) * (tq // tk), body, (m, l, acc))
    o_ref[...] = (acc / l).astype(o_ref.dtype)


def _prompt_attention(Q, K, V, B, SEQ, out_rows, meta_row0):
    H = Q.shape[0]
    tq = _tile(SEQ, (256, 128))
    tk = _tile(tq, (128,))
    nq = SEQ // tq
    mb = meta_row0 // N_META
    dk = NOPE_DIM + ROPE_DIM
    return pl.pallas_call(
        functools.partial(_attn_kernel, tq=tq, tk=tk), grid=(B, H, nq),
        in_specs=[pl.BlockSpec((1, tq, dk), lambda b, h, i: (h, b * nq + i, 0)),
                  pl.BlockSpec((1, SEQ, dk), lambda b, h, i: (h, b, 0)),
                  pl.BlockSpec((1, SEQ, V_DIM), lambda b, h, i: (h, b, 0)),
                  pl.BlockSpec((1, N_META, dk), lambda b, h, i: (h, mb + b, 0)),
                  pl.BlockSpec((1, N_META, V_DIM), lambda b, h, i: (h, mb + b, 0))],
        out_specs=pl.BlockSpec((tq, V_DIM), lambda b, h, i: (b * nq + i, h)),
        out_shape=jax.ShapeDtypeStruct((out_rows, H * V_DIM), BF16),
        compiler_params=_cp(3), name="mla_prompt_attention")(Q, K, V, K, V)


def _qlat_kernel(q_ref, wuk_ref, o_ref):
    q = q_ref[0]
    kr = wuk_ref.shape[1]
    o_ref[0, :, 0:kr] = _dot_nt(q[:, :NOPE_DIM], wuk_ref[0]).astype(BF16)
    o_ref[0, :, kr:kr + ROPE_DIM] = q[:, NOPE_DIM:]


def _q_latent(Q, wuk_t, row_blk, DB):
    H, _, KR = wuk_t.shape[0], None, wuk_t.shape[1]
    dk = NOPE_DIM + ROPE_DIM
    return pl.pallas_call(
        _qlat_kernel, grid=(H,),
        in_specs=[pl.BlockSpec((1, DB, dk), lambda h: (h, row_blk, 0)),
                  pl.BlockSpec((1, KR, NOPE_DIM), lambda h: (h, 0, 0))],
        out_specs=pl.BlockSpec((1, DB, KR + ROPE_DIM), lambda h: (h, 0, 0)),
        out_shape=jax.ShapeDtypeStruct((H, DB, KR + ROPE_DIM), BF16),
        compiler_params=_cp(1), name="mla_q_latent")(Q, wuk_t)


def _paged_kernel(pt_ref, q_ref, *rest, npg, kr):
    kv_refs = rest[:npg]
    kr_refs = rest[npg:2 * npg]
    kvs_ref, krs_ref, o_ref, m_s, l_s, acc_s = rest[2 * npg:]
    j = pl.program_id(1)

    @pl.when(j == 0)
    def _():
        m_s[...] = jnp.full(m_s.shape, -jnp.inf, F32)
        l_s[...] = jnp.zeros(l_s.shape, F32)
        acc_s[...] = jnp.zeros(acc_s.shape, F32)

    q = q_ref[0]
    ql = q[:, :kr]
    qr = q[:, kr:]
    scores, pages = [], []
    for p in range(npg):
        kv16 = kv_refs[p][0].astype(BF16)
        kr16 = kr_refs[p][0].astype(BF16)
        scores.append((_dot_nt(ql, kv16) + _dot_nt(qr, kr16)) * ATTN_SCALE)
        pages.append(kv16)
    m_prev = m_s[...]
    m_new = m_prev
    for s in scores:
        m_new = jnp.maximum(m_new, jnp.max(s, axis=-1, keepdims=True))
    corr = jnp.exp(m_prev - m_new)
    l = l_s[...] * corr
    acc = acc_s[...] * corr
    for s, kv16 in zip(scores, pages):
        p = jnp.exp(s - m_new)
        l = l + jnp.sum(p, axis=-1, keepdims=True)
        acc = acc + _dot(p.astype(BF16), kv16)
    m_s[...] = m_new
    l_s[...] = l
    acc_s[...] = acc

    @pl.when(j == pl.num_programs(1) - 1)
    def _():
        kvs = kvs_ref[0].astype(F32)
        krs = krs_ref[0].astype(F32)
        s = (jnp.sum(ql.astype(F32) * kvs, axis=-1, keepdims=True)
             + jnp.sum(qr.astype(F32) * krs, axis=-1, keepdims=True)) * ATTN_SCALE
        m2 = jnp.maximum(m_new, s)
        p = jnp.exp(s - m2)
        c2 = jnp.exp(m_new - m2)
        o_ref[0] = ((acc * c2 + p * kvs) / (l * c2 + p)).astype(o_ref.dtype)


def _paged_attention(page_table, Qs, cache_kv, cache_kr, kvs, krs):
    DB, H, dq = Qs.shape
    _, PAGE, KR = cache_kv.shape
    NP = page_table.shape[1]
    npg = _tile(NP, (PAGES_PER_STEP, 4, 2, 1))

    def page_map(p):
        return lambda b, j, pt: (pt[b, j * npg + p], 0, 0)

    in_specs = [pl.BlockSpec((1, H, dq), lambda b, j, pt: (b, 0, 0))]
    in_specs += [pl.BlockSpec((1, PAGE, KR), page_map(p)) for p in range(npg)]
    in_specs += [pl.BlockSpec((1, PAGE, ROPE_DIM), page_map(p)) for p in range(npg)]
    in_specs += [pl.BlockSpec((1, 1, KR), lambda b, j, pt: (b, 0, 0)),
                 pl.BlockSpec((1, 1, ROPE_DIM), lambda b, j, pt: (b, 0, 0))]
    return pl.pallas_call(
        functools.partial(_paged_kernel, npg=npg, kr=KR),
        grid_spec=pltpu.PrefetchScalarGridSpec(
            num_scalar_prefetch=1, grid=(DB, NP // npg), in_specs=in_specs,
            out_specs=pl.BlockSpec((1, H, KR), lambda b, j, pt: (b, 0, 0)),
            scratch_shapes=[pltpu.VMEM((H, 1), F32), pltpu.VMEM((H, 1), F32), pltpu.VMEM((H, KR), F32)]),
        out_shape=jax.ShapeDtypeStruct((DB, H, KR), BF16),
        compiler_params=_cp(2), name="mla_paged_attention")(
            page_table, Qs, *([cache_kv] * npg), *([cache_kr] * npg), kvs, krs)


def _ov_kernel(a_ref, w_ref, o_ref):
    o_ref[...] = _dot(a_ref[0], w_ref[0]).astype(o_ref.dtype)


def _sample_values(acc_t, wuv_t):
    H, DB, KR = acc_t.shape
    return pl.pallas_call(
        _ov_kernel, grid=(H,),
        in_specs=[pl.BlockSpec((1, DB, KR), lambda h: (h, 0, 0)),
                  pl.BlockSpec((1, KR, V_DIM), lambda h: (h, 0, 0))],
        out_specs=pl.BlockSpec((DB, V_DIM), lambda h: (0, h)),
        out_shape=jax.ShapeDtypeStruct((DB, H * V_DIM), BF16),
        compiler_params=_cp(1), name="mla_sample_values")(acc_t, wuv_t)


def _mlstm_kernel(bif_ref, q_ref, k_ref, v_ref, og_ref, gc_ref, gi_ref, gf_ref, mng_ref, c0_ref, n0_ref, m0_ref,
                  om_ref, co_ref, no_ref, mo_ref, c_s, n_s, m_s, *, L, nc, dqk):
    h = pl.program_id(1)
    c = pl.program_id(2)

    @pl.when(c == 0)
    def _():
        c_s[...] = c0_ref[0, 0]
        n_s[...] = n0_ref[0, 0]
        m_s[...] = m0_ref[0, 0]

    b_i = bif_ref[h]
    b_f = bif_ref[H_M + h]
    gc = gc_ref[...]
    lane = lax.broadcasted_iota(jnp.int32, gc.shape, 1)
    li_c = jnp.sum(jnp.where(lane == h, gc, 0.0), axis=1, keepdims=True) + b_i
    lf_c = _log_sigmoid(jnp.sum(jnp.where(lane == H_M + h, gc, 0.0), axis=1, keepdims=True) + b_f)
    li_r = gi_ref[0, 0] + b_i
    lf_r = _log_sigmoid(gf_ref[0, 0] + b_f)

    row = lax.broadcasted_iota(jnp.int32, (L, L), 0)
    col = lax.broadcasted_iota(jnp.int32, (L, L), 1)
    tri = col <= row
    f_c = jnp.sum(jnp.where(tri, lf_r, 0.0), axis=1, keepdims=True)
    f_r = jnp.sum(jnp.where(row <= col, lf_c, 0.0), axis=0, keepdims=True)
    m_prev = m_s[...]
    dmat = jnp.where(tri, f_c - f_r + li_r, -jnp.inf)
    b = f_c + m_prev
    m_row = jnp.maximum(b, jnp.max(dmat, axis=1, keepdims=True))

    qs = q_ref[...] * (dqk ** -0.5)
    k = k_ref[...]
    v = v_ref[...]
    q16 = qs.astype(BF16)
    k16 = k.astype(BF16)
    w = _dot_nt(q16, k16) * jnp.exp(dmat - m_row)
    bw = jnp.exp(b - m_row)
    c_old = c_s[...]
    n_old = n_s[...]
    num = bw * _dot_nt(q16, c_old.astype(BF16)) + _dot(w.astype(BF16), v.astype(BF16))
    den = bw * jnp.sum(n_old * qs, axis=1, keepdims=True) + jnp.sum(w, axis=1, keepdims=True)
    hc = num / jnp.maximum(jnp.abs(den), jnp.exp(-m_row))

    m_new = m_row[L - 1:L, :]
    f_last = f_c[L - 1:L, :]
    w_s = jnp.exp(f_last - f_c + li_c - m_new)
    decay = jnp.exp(f_last + m_prev - m_new)
    c_s[...] = decay * c_old + _dot_tn((v * w_s).astype(BF16), k16)
    n_s[...] = decay * n_old + jnp.sum(w_s * k, axis=0, keepdims=True)
    m_s[...] = m_new

    hn = hc * lax.rsqrt(jnp.mean(hc * hc, axis=-1, keepdims=True) + EPS) * mng_ref[...]
    om_ref[...] = (hn * _sigmoid(og_ref[...])).astype(om_ref.dtype)

    @pl.when(c == nc - 1)
    def _():
        co_ref[0, 0] = c_s[...]
        no_ref[0, 0] = n_s[...]
        mo_ref[0, 0] = m_s[...]


def _mlstm_chunks(z, g_rows, b_if, m_norm_g, c0, n0, m0, off, *, B, L, nc, row0, out_rows, dqk, dv):
    rb0 = row0 // L

    def rb(b, c):
        return rb0 + b * nc + c

    state = lambda shp: pl.BlockSpec((1, 1) + shp, lambda b, h, c: (b, h, 0, 0))
    return pl.pallas_call(
        functools.partial(_mlstm_kernel, L=L, nc=nc, dqk=dqk), grid=(B, H_M, nc),
        in_specs=[pl.BlockSpec(memory_space=pltpu.SMEM),
                  pl.BlockSpec((L, dqk), lambda b, h, c: (rb(b, c), off["mq"] // dqk + h)),
                  pl.BlockSpec((L, dqk), lambda b, h, c: (rb(b, c), off["mk"] // dqk + h)),
                  pl.BlockSpec((L, dv), lambda b, h, c: (rb(b, c), off["mv"] // dv + h)),
                  pl.BlockSpec((L, dv), lambda b, h, c: (rb(b, c), off["mo"] // dv + h)),
                  pl.BlockSpec((L, LANES), lambda b, h, c: (rb(b, c), off["gates"] // LANES)),
                  pl.BlockSpec((1, 1, 1, L), lambda b, h, c: (rb(b, c), h, 0, 0)),
                  pl.BlockSpec((1, 1, 1, L), lambda b, h, c: (rb(b, c), H_M + h, 0, 0)),
                  pl.BlockSpec((1, dv), lambda b, h, c: (0, h)),
                  state((dv, dqk)), state((1, dqk)), state((1, 1))],
        out_specs=[pl.BlockSpec((L, dv), lambda b, h, c: (b * nc + c, h)),
                   state((dv, dqk)), state((1, dqk)), state((1, 1))],
        out_shape=[jax.ShapeDtypeStruct((out_rows, H_M * dv), BF16),
                   jax.ShapeDtypeStruct((B, H_M, dv, dqk), F32),
                   jax.ShapeDtypeStruct((B, H_M, 1, dqk), F32),
                   jax.ShapeDtypeStruct((B, H_M, 1, 1), F32)],
        scratch_shapes=[pltpu.VMEM((dv, dqk), F32), pltpu.VMEM((1, dqk), F32), pltpu.VMEM((1, 1), F32)],
        compiler_params=_cp(3), name=f"mlstm_chunks_L{L}")(
            b_if, z, z, z, z, z, g_rows, g_rows, m_norm_g.reshape(1, H_M * dv), c0, n0, m0)


def _mlstm_step_kernel(q_ref, k_ref, v_ref, og_ref, g_ref, bif_ref, mng_ref, c_ref, n_ref, m_ref,
                       om_ref, co_ref, no_ref, mo_ref, *, dqk):
    g = g_ref[0] + bif_ref[...]
    li = g[0:H_M]
    lf = _log_sigmoid(g[H_M:2 * H_M])
    m_prev = m_ref[0]
    qs = q_ref[0] * (dqk ** -0.5)
    k = k_ref[0]
    v = v_ref[0]
    n_old = n_ref[0]
    b = lf + m_prev
    m_row = jnp.maximum(b, li)
    q16 = qs.astype(BF16)
    k16 = k.astype(BF16)
    v16 = v.astype(BF16)
    w = jnp.sum(q16.astype(F32) * k16.astype(F32), axis=1, keepdims=True) * jnp.exp(li - m_row)
    bw = jnp.exp(b - m_row)
    den = bw * jnp.sum(n_old * qs, axis=1, keepdims=True) + w
    dn = jnp.maximum(jnp.abs(den), jnp.exp(-m_row))
    w_s = jnp.exp(li - m_row)
    decay = jnp.exp(lf + m_prev - m_row)
    no_ref[0] = decay * n_old + w_s * k
    mo_ref[0] = m_row
    vw = v * w_s
    zeros_v = jnp.zeros_like(vw)
    k_pad = jnp.concatenate([k, jnp.zeros_like(k)], axis=0).astype(BF16)
    hrow = lax.broadcasted_iota(jnp.int32, vw.shape, 0)
    outs = []
    for h in range(H_M):
        c_old = c_ref[0, h]
        cq = _dot_nt(q16, c_old.astype(BF16))[h:h + 1]
        num = bw[h:h + 1] * cq + w[h:h + 1] * v16[h:h + 1].astype(F32)
        outs.append(num / dn[h:h + 1])
        vw_h = jnp.concatenate([jnp.where(hrow == h, vw, 0.0), zeros_v], axis=0).astype(BF16)
        co_ref[0, h] = decay[h:h + 1] * c_old + _dot_tn(vw_h, k_pad)
    hc = jnp.concatenate(outs, axis=0)
    hn = hc * lax.rsqrt(jnp.mean(hc * hc, axis=-1, keepdims=True) + EPS) * mng_ref[...]
    om_ref[0] = hn * _sigmoid(og_ref[0])


def _mlstm_step(q, k, v, og, g, b_if, m_norm_g, c, n, m):
    DB, _, dqk = q.shape
    dv = v.shape[2]
    per_b = lambda *shp: pl.BlockSpec((1,) + shp, lambda b: (b,) + (0,) * len(shp))
    full = lambda *shp: pl.BlockSpec(shp, lambda b: (0,) * len(shp))
    return pl.pallas_call(
        functools.partial(_mlstm_step_kernel, dqk=dqk), grid=(DB,),
        in_specs=[per_b(H_M, dqk), per_b(H_M, dqk), per_b(H_M, dv), per_b(H_M, dv), per_b(2 * H_M, 1),
                  full(2 * H_M, 1), full(H_M, dv), per_b(H_M, dv, dqk), per_b(H_M, dqk), per_b(H_M, 1)],
        out_specs=[per_b(H_M, dv), per_b(H_M, dv, dqk), per_b(H_M, dqk), per_b(H_M, 1)],
        out_shape=[jax.ShapeDtypeStruct((DB, H_M, dv), F32), jax.ShapeDtypeStruct((DB, H_M, dv, dqk), F32),
                   jax.ShapeDtypeStruct((DB, H_M, dqk), F32), jax.ShapeDtypeStruct((DB, H_M, 1), F32)],
        compiler_params=_cp(1), name="mlstm_step")(
            q, k, v, og, g, b_if.reshape(2 * H_M, 1), m_norm_g.reshape(H_M, dv), c, n, m)


def _merge_kernel(oa_ref, om_ref, wa_ref, wm_ref, ga_ref, gm_ref, o_ref):
    a = _dot(oa_ref[...], wa_ref[...])
    m = _dot(om_ref[...], wm_ref[...])
    o_ref[...] = (_sigmoid(ga_ref[...]) * a + _sigmoid(gm_ref[...]) * m).astype(o_ref.dtype)


def _merge(o_a, o_m, w_br_a, w_br_m, z, off):
    M, Ka = o_a.shape
    Km = o_m.shape[1]
    N = w_br_a.shape[1]
    tm = _tile(M, (640, 320, 256, 128, 64, 16))
    tn = _tile(N, (512, 256, 128))
    return pl.pallas_call(
        _merge_kernel, grid=(M // tm, N // tn),
        in_specs=[pl.BlockSpec((tm, Ka), lambda i, j: (i, 0)),
                  pl.BlockSpec((tm, Km), lambda i, j: (i, 0)),
                  pl.BlockSpec((Ka, tn), lambda i, j: (0, j)),
                  pl.BlockSpec((Km, tn), lambda i, j: (0, j)),
                  pl.BlockSpec((tm, tn), lambda i, j: (i, off["ga"] // tn + j)),
                  pl.BlockSpec((tm, tn), lambda i, j: (i, off["gm"] // tn + j))],
        out_specs=pl.BlockSpec((tm, tn), lambda i, j: (i, j)),
        out_shape=jax.ShapeDtypeStruct((M, N), BF16),
        compiler_params=_cp(2), name="branch_merge")(o_a, o_m, w_br_a, w_br_m, z, z)


def _ln_router_kernel(h_ref, mix_ref, g_ref, b_ref, wr_ref, br_ref, o32_ref, ei_ref, pr_ref, *, n_exp):
    y = _layernorm(ALPHA * h_ref[...] + mix_ref[...], g_ref[...], b_ref[...])
    o32_ref[...] = y
    logits = _dot(y.astype(BF16), wr_ref[...]) + br_ref[...]
    lane = lax.broadcasted_iota(jnp.int32, logits.shape, 1)
    logits = jnp.where(lane < n_exp, logits, -jnp.inf)
    vals, idxs = [], []
    for _ in range(TOP_K):
        mx = jnp.max(logits, axis=1, keepdims=True)
        ix = jnp.min(jnp.where(logits == mx, lane, LANES), axis=1, keepdims=True)
        vals.append(mx)
        idxs.append(ix)
        logits = jnp.where(lane == ix, -jnp.inf, logits)
    es = [jnp.exp(vv - vals[0]) for vv in vals]
    tot = es[0]
    for e in es[1:]:
        tot = tot + e
    ei = jnp.zeros(lane.shape, jnp.int32)
    pr = jnp.zeros(lane.shape, F32)
    for kk in range(TOP_K):
        ei = jnp.where(lane == kk, idxs[kk], ei)
        pr = jnp.where(lane == kk, es[kk] / tot, pr)
    ei_ref[...] = ei
    pr_ref[...] = pr


def _ln_router(h32, mix, g, b, w_router, b_router):
    M, D = mix.shape
    E = w_router.shape[1]
    wr = jnp.zeros((D, LANES), BF16).at[:, :E].set(w_router.astype(BF16))
    br = jnp.zeros((1, LANES), F32).at[0, :E].set(b_router)
    tm = _tile(M, (128, 64, 32, 16, 8))
    row = pl.BlockSpec((tm, D), lambda i: (i, 0))
    vec = pl.BlockSpec((1, D), lambda i: (0, 0))
    nar = pl.BlockSpec((tm, LANES), lambda i: (i, 0))
    return pl.pallas_call(
        functools.partial(_ln_router_kernel, n_exp=E), grid=(M // tm,),
        in_specs=[row, row, vec, vec, pl.BlockSpec((D, LANES), lambda i: (0, 0)),
                  pl.BlockSpec((1, LANES), lambda i: (0, 0))],
        out_specs=[row, nar, nar],
        out_shape=[jax.ShapeDtypeStruct((M, D), F32),
                   jax.ShapeDtypeStruct((M, LANES), jnp.int32), jax.ShapeDtypeStruct((M, LANES), F32)],
        compiler_params=_cp(1), name="ln1_router")(h32, mix, g.reshape(1, D), b.reshape(1, D), wr, br)


def _gather_kernel(valid_ref, tok_ref, x_ref, o_ref, buf, sem):
    i = pl.program_id(0)

    @pl.when(valid_ref[i] > 0)
    def _():
        def copy(r):
            return pltpu.make_async_copy(x_ref.at[pl.ds(tok_ref[0, 0, r], 1), :], buf.at[pl.ds(r, 1), :], sem)

        def start(r, _):
            copy(r).start()
            return 0

        def wait(r, _):
            copy(r).wait()
            return 0

        lax.fori_loop(0, MOE_SUB, start, 0)
        lax.fori_loop(0, MOE_SUB, wait, 0)
        o_ref[...] = buf[...].astype(o_ref.dtype)

    @pl.when(valid_ref[i] == 0)
    def _():
        o_ref[...] = jnp.zeros(o_ref.shape, o_ref.dtype)


def _dispatch(x32, slot_tok, sub_valid, n_slots):
    D = x32.shape[1]
    nsub = n_slots // MOE_SUB
    return pl.pallas_call(
        _gather_kernel,
        grid_spec=pltpu.PrefetchScalarGridSpec(
            num_scalar_prefetch=1, grid=(nsub,),
            in_specs=[pl.BlockSpec((1, 1, MOE_SUB), lambda i, v: (i, 0, 0), memory_space=pltpu.SMEM),
                      pl.BlockSpec(memory_space=pl.ANY)],
            out_specs=pl.BlockSpec((MOE_SUB, D), lambda i, v: (i, 0)),
            scratch_shapes=[pltpu.VMEM((MOE_SUB, D), F32), pltpu.SemaphoreType.DMA(())]),
        out_shape=jax.ShapeDtypeStruct((n_slots, D), BF16),
        compiler_params=_cp(1), name="moe_dispatch")(
            sub_valid, slot_tok.reshape(nsub, 1, MOE_SUB), x32)


def _ffn_up_kernel(se_ref, sb_ref, sv_ref, *rest):
    x_refs = rest[:MOE_NSUB]
    wg_ref, wu_ref, bg_ref, bu_ref, o_ref = rest[MOE_NSUB:]
    s = pl.program_id(0)
    wg = wg_ref[0].astype(BF16)
    wu = wu_ref[0].astype(BF16)
    bg = bg_ref[0]
    bu = bu_ref[0]
    for t in range(MOE_NSUB):
        rows = pl.ds(t * MOE_SUB, MOE_SUB)

        @pl.when(sv_ref[s * MOE_NSUB + t] > 0)
        def _():
            x = x_refs[t][...]
            g = jnp.minimum(_dot(x, wg) + bg, SWIGLU_LIMIT)
            u = jnp.clip(_dot(x, wu) + bu, -SWIGLU_LIMIT, SWIGLU_LIMIT)
            o_ref[rows, :] = (g * _sigmoid(SWIGLU_ALPHA * g) * (u + 1.0)).astype(o_ref.dtype)

        @pl.when(sv_ref[s * MOE_NSUB + t] == 0)
        def _():
            o_ref[rows, :] = jnp.zeros((MOE_SUB, o_ref.shape[1]), o_ref.dtype)


def _weight_chunk(n_chunks):
    def f(s, c, se, sb, sv):
        return (se[s], 0, jnp.where(sv[s * MOE_NSUB] > 0, c, n_chunks - 1))
    return f


def _ffn_up(st_e, st_blk, sub_valid, xs, w_gate, w_up, b_gate, b_up, n_super):
    E, D, F = w_gate.shape
    tf = _tile(F, (256, 128))

    def x_map(t):
        def f(s, c, se, sb, sv):
            return (sb[s] * MOE_NSUB + jnp.where(sv[s * MOE_NSUB + t] > 0, t, 0), 0)
        return f

    w_spec = pl.BlockSpec((1, D, tf), _weight_chunk(F // tf))
    b_spec = pl.BlockSpec((1, 1, tf), _weight_chunk(F // tf))
    return pl.pallas_call(
        _ffn_up_kernel,
        grid_spec=pltpu.PrefetchScalarGridSpec(
            num_scalar_prefetch=3, grid=(n_super, F // tf),
            in_specs=[pl.BlockSpec((MOE_SUB, D), x_map(t)) for t in range(MOE_NSUB)]
            + [w_spec, w_spec, b_spec, b_spec],
            out_specs=pl.BlockSpec((MOE_SUPER, tf), lambda s, c, se, sb, sv: (s, c))),
        out_shape=jax.ShapeDtypeStruct((n_super * MOE_SUPER, F), BF16),
        compiler_params=_cp(2), name="moe_ffn_up")(
            st_e, st_blk, sub_valid, *([xs] * MOE_NSUB), w_gate, w_up,
            b_gate.reshape(E, 1, F), b_up.reshape(E, 1, F))


def _ffn_down_kernel(se_ref, sb_ref, sv_ref, a_ref, wd_ref, bd_ref, o_ref):
    s = pl.program_id(0)
    wd = wd_ref[0].astype(BF16)
    bd = bd_ref[0]
    for t in range(MOE_NSUB):
        rows = pl.ds(t * MOE_SUB, MOE_SUB)

        @pl.when(sv_ref[s * MOE_NSUB + t] > 0)
        def _():
            o_ref[rows, :] = _dot(a_ref[rows, :], wd) + bd

        @pl.when(sv_ref[s * MOE_NSUB + t] == 0)
        def _():
            o_ref[rows, :] = jnp.zeros((MOE_SUB, o_ref.shape[1]), o_ref.dtype)


def _ffn_down(st_e, st_blk, sub_valid, act, w_down, b_down, n_super):
    E, F, D = w_down.shape
    tn = _tile(D, (256, 128))
    return pl.pallas_call(
        _ffn_down_kernel,
        grid_spec=pltpu.PrefetchScalarGridSpec(
            num_scalar_prefetch=3, grid=(n_super, D // tn),
            in_specs=[pl.BlockSpec((MOE_SUPER, F), lambda s, c, se, sb, sv: (sb[s], 0)),
                      pl.BlockSpec((1, F, tn), _weight_chunk(D // tn)),
                      pl.BlockSpec((1, 1, tn), _weight_chunk(D // tn))],
            out_specs=pl.BlockSpec((MOE_SUPER, tn), lambda s, c, se, sb, sv: (s, c))),
        out_shape=jax.ShapeDtypeStruct((n_super * MOE_SUPER, D), F32),
        compiler_params=_cp(2), name="moe_ffn_down")(
            st_e, st_blk, sub_valid, act, w_down, b_down.reshape(E, 1, D))


def _combine_kernel(slot_ref, ys_ref, pr_ref, h_ref, g_ref, b_ref, o_ref, buf, sem, *, tt):
    def copy(r):
        k = r // tt
        t = r - k * tt
        return pltpu.make_async_copy(ys_ref.at[pl.ds(slot_ref[0, 0, r], 1), :],
                                     buf.at[k, pl.ds(t, 1), :], sem)

    def start(r, _):
        copy(r).start()
        return 0

    def wait(r, _):
        copy(r).wait()
        return 0

    lax.fori_loop(0, TOP_K * tt, start, 0)
    lax.fori_loop(0, TOP_K * tt, wait, 0)
    pr = pr_ref[...]
    y = pr[:, 0:1] * buf[0]
    for k in range(1, TOP_K):
        y = y + pr[:, k:k + 1] * buf[k]
    o_ref[...] = _layernorm(ALPHA * h_ref[...] + y, g_ref[...], b_ref[...])


def _combine(slots, ys, probs, h32, g, b):
    M, D = h32.shape
    tt = _tile(M, (64, 32, 16, 8))
    row = pl.BlockSpec((tt, D), lambda i: (i, 0))
    vec = pl.BlockSpec((1, D), lambda i: (0, 0))
    return pl.pallas_call(
        functools.partial(_combine_kernel, tt=tt), grid=(M // tt,),
        in_specs=[pl.BlockSpec((1, 1, TOP_K * tt), lambda i: (i, 0, 0), memory_space=pltpu.SMEM),
                  pl.BlockSpec(memory_space=pl.ANY),
                  pl.BlockSpec((tt, LANES), lambda i: (i, 0)), row, vec, vec],
        out_specs=row,
        out_shape=jax.ShapeDtypeStruct((M, D), F32),
        scratch_shapes=[pltpu.VMEM((TOP_K, tt, D), F32), pltpu.SemaphoreType.DMA(())],
        compiler_params=_cp(1), name="moe_combine_ln2")(
            slots, ys, probs, h32, g.reshape(1, D), b.reshape(1, D))


def _moe_plan(eid, n_exp, n_super):
    M = eid.shape[0]
    S = M * TOP_K
    ef = eid.reshape(S)
    onehot = (ef[:, None] == jnp.arange(n_exp, dtype=jnp.int32)[None, :]).astype(jnp.int32)
    seen = jnp.cumsum(onehot, axis=0)
    cnt = seen[-1]
    rank = jnp.sum((seen - onehot) * onehot, axis=1)
    nsup = (cnt + MOE_SUPER - 1) // MOE_SUPER
    sup_end = jnp.cumsum(nsup)
    sup_start = sup_end - nsup
    slot_of = sup_start[ef] * MOE_SUPER + rank
    n_slots = n_super * MOE_SUPER
    slot_tok = jnp.zeros((n_slots,), jnp.int32).at[slot_of].set(jnp.arange(S, dtype=jnp.int32) // TOP_K)
    total = sup_end[-1]
    sidx = jnp.arange(n_super, dtype=jnp.int32)
    used = sidx < total
    s_eff = jnp.minimum(sidx, total - 1)
    st_e = jnp.minimum(jnp.searchsorted(sup_end, s_eff, side="right"), n_exp - 1).astype(jnp.int32)
    rows = jnp.where(used, jnp.clip(cnt[st_e] - (s_eff - sup_start[st_e]) * MOE_SUPER, 0, MOE_SUPER), 0)
    sub_valid = (rows[:, None] > jnp.arange(MOE_NSUB, dtype=jnp.int32)[None, :] * MOE_SUB)
    return (slot_tok, slot_of, st_e, s_eff.astype(jnp.int32),
            sub_valid.astype(jnp.int32).reshape(n_super * MOE_NSUB))


def kernel(x_prompt, x_sample, cache_kv, cache_kr, state_C, state_n, state_m, page_table, meta, ln_in_g, ln_in_b,
           w_in, b_if, q_norm_g, w_uq, kv_norm_g, w_uk, w_uv, m_norm_g, w_br_a, w_br_m, w_out, ln1_g, ln1_b,
           w_router, b_router, w_gate, b_gate, w_up, b_up, w_down, b_down, ln2_g, ln2_b):
    B, SEQ, D = x_prompt.shape
    DB, DS, _ = x_sample.shape
    assert DS == 1 and w_in.shape[0] == DEPTH == 1
    NP = page_table.shape[1]
    PAGE = cache_kv.shape[2]
    QR, KR = q_norm_g.shape[1], kv_norm_g.shape[1]
    E = w_router.shape[2]
    dqk, dv = D // 2 // H_M, D // H_M
    T = B * SEQ
    M = T + DB
    meta0 = M
    R = -(-(M + B * N_META) // ROW_ALIGN) * ROW_ALIGN
    assert T % DB == 0 and M % N_META == 0 and SEQ % CHUNK == 0

    w = w_in[0]
    sizes = (QR, KR, ROPE_DIM, H_M * dqk, H_M * dqk, H_M * dv, H_M * dv, H_M, H_M, D, D)
    starts = np.concatenate([[0], np.cumsum(sizes)]).tolist()
    seg = {n: w[:, starts[i]:starts[i + 1]] for i, n in enumerate(
        ("cq", "ckv", "kr", "mq", "mk", "mv", "mo", "mi", "mf", "ga", "gm"))}

    def half_swap(wr):
        hd = ROPE_DIM // 2
        return jnp.concatenate([-wr[..., hd:], wr[..., :hd]], axis=-1)

    groups = [("mq", seg["mq"]), ("mk", seg["mk"]), ("mv", seg["mv"]), ("mo", seg["mo"]), ("ga", seg["ga"]),
              ("gm", seg["gm"]), ("cq", seg["cq"]), ("ckv", seg["ckv"]),
              ("kr", jnp.concatenate([seg["kr"], half_swap(seg["kr"])], axis=1)),
              ("gates", jnp.concatenate([seg["mi"], seg["mf"], jnp.zeros((D, LANES - 2 * H_M), F32)], axis=1))]
    off, pos = {}, 0
    for n, g in groups:
        off[n] = pos
        pos += g.shape[1]
    n2 = -(-pos // 1024) * 1024
    w2 = jnp.concatenate([g for _, g in groups] + [jnp.zeros((D, n2 - pos), F32)], axis=1).astype(BF16)
    for n, blk in (("mq", dqk), ("mk", dqk), ("mv", dv), ("mo", dv), ("cq", QR), ("ckv", KR), ("kr", LANES),
                   ("gates", LANES), ("ga", 512), ("gm", 512)):
        assert off[n] % blk == 0, (n, off[n], blk)

    dk = NOPE_DIM + ROPE_DIM
    wq = w_uq[0].reshape(QR, H_A, dk)
    wq_r = wq[..., NOPE_DIM:]
    w_uq3 = jnp.transpose(jnp.concatenate([wq, half_swap(wq_r)], axis=-1), (1, 0, 2)).astype(BF16)
    wuk_t = jnp.transpose(w_uk[0], (1, 0, 2)).astype(BF16)
    wuv_t = jnp.transpose(w_uv[0], (1, 0, 2)).astype(BF16)

    pos_rows = np.zeros((R,), np.float32)
    pos_rows[:T] = np.tile(N_META + np.arange(SEQ), B)
    pos_rows[T:M] = NP * PAGE
    pos_rows[meta0:meta0 + B * N_META] = np.tile(np.arange(N_META), B)
    inv = 1.0 / (ROPE_THETA ** (jnp.arange(0, ROPE_DIM, 2, dtype=F32) / ROPE_DIM))
    ang = jnp.asarray(pos_rows)[:, None] * inv
    cos_t = jnp.tile(jnp.cos(ang), (1, LANES // (ROPE_DIM // 2)))
    sin_t = jnp.tile(jnp.sin(ang), (1, LANES // (ROPE_DIM // 2)))

    x_all = jnp.concatenate([x_prompt.reshape(T, D), x_sample.reshape(DB, D),
                             jnp.tile(meta.astype(F32), (B, 1)),
                             jnp.zeros((R - M - B * N_META, D), F32)], axis=0)
    h32, h16 = _ln_in(x_all, ln_in_g, ln_in_b)
    z = _matmul(h16, w2, F32, name="in_proj")

    cqn, kv32, kv16, kr32, kr16 = _post(z, cos_t, sin_t, q_norm_g[0], kv_norm_g[0], off)
    Q = _queries(cqn, w_uq3, cos_t, sin_t)
    K, V = _kv_expand(kv16, kr16, wuk_t, wuv_t)

    o_a_p = _prompt_attention(Q, K, V, B, SEQ, T, meta0)
    qs = jnp.transpose(_q_latent(Q, wuk_t, T // DB, DB), (1, 0, 2))
    acc = _paged_attention(page_table, qs, cache_kv[0], cache_kr[0],
                           kv16[T:M].reshape(DB, 1, KR), kr16[T:M].reshape(DB, 1, ROPE_DIM))
    o_a_s = _sample_values(jnp.transpose(acc, (1, 0, 2)), wuv_t)
    o_a = jnp.concatenate([o_a_p, o_a_s], axis=0)

    gates = z[:, off["gates"]:off["gates"] + 2 * H_M]

    def g_rows(L):
        return jnp.transpose(gates.reshape(R // L, L, 2 * H_M), (0, 2, 1)).reshape(R // L, 2 * H_M, 1, L)

    zc = jnp.zeros((B, H_M, dv, dqk), F32)
    zn = jnp.zeros((B, H_M, 1, dqk), F32)
    zm = jnp.zeros((B, H_M, 1, 1), F32)
    _, c1, n1, m1 = _mlstm_chunks(z, g_rows(N_META), b_if[0], m_norm_g[0], zc, zn, zm, off,
                                  B=B, L=N_META, nc=1, row0=meta0, out_rows=B * N_META, dqk=dqk, dv=dv)
    o_m_p, c_p, n_p, m_p = _mlstm_chunks(z, g_rows(CHUNK), b_if[0], m_norm_g[0], c1, n1, m1, off,
                                         B=B, L=CHUNK, nc=SEQ // CHUNK, row0=0, out_rows=T, dqk=dqk, dv=dv)
    zs = z[T:M]
    o_m_s, c_s, n_s, m_s = _mlstm_step(
        zs[:, off["mq"]:off["mq"] + H_M * dqk].reshape(DB, H_M, dqk),
        zs[:, off["mk"]:off["mk"] + H_M * dqk].reshape(DB, H_M, dqk),
        zs[:, off["mv"]:off["mv"] + H_M * dv].reshape(DB, H_M, dv),
        zs[:, off["mo"]:off["mo"] + H_M * dv].reshape(DB, H_M, dv),
        zs[:, off["gates"]:off["gates"] + 2 * H_M].reshape(DB, 2 * H_M, 1),
        b_if[0], m_norm_g[0], state_C[0], state_n[0], state_m[0].reshape(DB, H_M, 1))
    o_m = jnp.concatenate([o_m_p, o_m_s.reshape(DB, H_M * dv).astype(BF16)], axis=0)

    merged = _merge(o_a, o_m, w_br_a[0].astype(BF16), w_br_m[0].astype(BF16), z, off)
    mix = _matmul(merged, w_out[0].astype(BF16), F32, name="out_proj")
    hp32, eid, probs = _ln_router(h32, mix, ln1_g[0], ln1_b[0], w_router[0], b_router[0])

    n_super = (M * TOP_K) // MOE_SUPER + E
    slot_tok, slot_of, st_e, st_blk, sub_valid = _moe_plan(eid[:, :TOP_K], E, n_super)
    xs = _dispatch(hp32, slot_tok, sub_valid, n_super * MOE_SUPER)
    act = _ffn_up(st_e, st_blk, sub_valid, xs, w_gate[0], w_up[0], b_gate[0], b_up[0], n_super)
    ys = _ffn_down(st_e, st_blk, sub_valid, act, w_down[0], b_down[0], n_super)
    tt = _tile(M, (64, 32, 16, 8))
    slots = jnp.transpose(slot_of.reshape(M // tt, tt, TOP_K), (0, 2, 1)).reshape(M // tt, 1, TOP_K * tt)
    y = _combine(slots, ys, probs, hp32, ln2_g[0], ln2_b[0])

    def with_meta(a):
        return jnp.concatenate([a[meta0:meta0 + B * N_META].reshape(B, N_META, -1),
                                a[:T].reshape(B, SEQ, -1)], axis=1)[None]

    return (y[:T].reshape(B, SEQ, D), y[T:M].reshape(DB, 1, D),
            with_meta(kv32), with_meta(kr32),
            c_p[None], n_p.reshape(1, B, H_M, dqk), m_p.reshape(1, B, H_M),
            kv32[T:M].reshape(1, DB, 1, KR), kr32[T:M].reshape(1, DB, 1, ROPE_DIM),
            c_s[None], n_s[None], m_s.reshape(1, DB, H_M))
```

```python
import functools

import jax
import jax.numpy as jnp
import numpy as np
from jax import lax
from jax.experimental import pallas as pl
from jax.experimental.pallas import tpu as pltpu

F32 = jnp.float32
BF16 = jnp.bfloat16

N_META = 16
H_A = 32
NOPE_DIM = 128
ROPE_DIM = 64
V_DIM = 128
ROPE_THETA = 10000.0
ATTN_SCALE = (NOPE_DIM + ROPE_DIM) ** -0.5
H_M = 8
CHUNK = 64
TOP_K = 4
SWIGLU_LIMIT = 7.0
SWIGLU_ALPHA = 1.702
DEPTH = 1
ALPHA = (2 * DEPTH) ** 0.25
EPS = 1e-5

LANES = 128
VMEM_LIMIT = 52 * 1024 * 1024
ROW_ALIGN = 256
MOE_SUPER = 1280
MOE_SUB = 256
MOE_NSUB = MOE_SUPER // MOE_SUB
PAGES_PER_STEP = 32


def _cp(n_axes):
    return pltpu.CompilerParams(dimension_semantics=("arbitrary",) * n_axes, vmem_limit_bytes=VMEM_LIMIT)


def _tile(n, prefs):
    for t in prefs:
        if n % t == 0:
            return t
    return n


def _dot(a, b):
    return jnp.dot(a, b, preferred_element_type=F32)


def _dot_nt(a, b):
    return lax.dot_general(a, b, (((1,), (1,)), ((), ())), preferred_element_type=F32)


def _dot_tn(a, b):
    return lax.dot_general(a, b, (((0,), (0,)), ((), ())), preferred_element_type=F32)


def _log_sigmoid(x):
    return jnp.minimum(x, 0.0) - jnp.log(1.0 + jnp.exp(-jnp.abs(x)))


def _sigmoid(x):
    return 1.0 / (1.0 + jnp.exp(-x))


def _layernorm(x, g, b):
    mu = jnp.mean(x, axis=-1, keepdims=True)
    xc = x - mu
    var = jnp.mean(xc * xc, axis=-1, keepdims=True)
    return xc * lax.rsqrt(var + EPS) * g + b


def _rmsnorm(x, g):
    return x * lax.rsqrt(jnp.mean(x * x, axis=-1, keepdims=True) + EPS) * g


def _ln_in_kernel(xp_ref, xr_ref, g_ref, b_ref, o32_ref, o16_ref, *, n_prompt_blocks):
    def emit(x_ref):
        y = _layernorm(x_ref[...], g_ref[...], b_ref[...])
        o32_ref[...] = y
        o16_ref[...] = y.astype(BF16)

    @pl.when(pl.program_id(0) < n_prompt_blocks)
    def _():
        emit(xp_ref)

    @pl.when(pl.program_id(0) >= n_prompt_blocks)
    def _():
        emit(xr_ref)


def _ln_in(x_prompt, x_rest, g, b):
    T, D = x_prompt.shape
    R = T + x_rest.shape[0]
    tm = _tile(np.gcd(T, x_rest.shape[0]), (128, 64, 32, 16, 8))
    npb = T // tm
    row = pl.BlockSpec((tm, D), lambda i: (i, 0))
    vec = pl.BlockSpec((1, D), lambda i: (0, 0))
    return pl.pallas_call(
        functools.partial(_ln_in_kernel, n_prompt_blocks=npb), grid=(R // tm,),
        in_specs=[pl.BlockSpec((tm, D), lambda i: (jnp.minimum(i, npb - 1), 0)),
                  pl.BlockSpec((tm, D), lambda i: (jnp.maximum(i - npb, 0), 0)), vec, vec],
        out_specs=[row, row],
        out_shape=[jax.ShapeDtypeStruct((R, D), F32), jax.ShapeDtypeStruct((R, D), BF16)],
        compiler_params=_cp(1), name="ln_in")(x_prompt, x_rest, g.reshape(1, D), b.reshape(1, D))


def _mm_kernel(a_ref, b_ref, o_ref, *, transposed_b):
    dot = _dot_nt if transposed_b else _dot
    o_ref[...] = dot(a_ref[...], b_ref[...]).astype(o_ref.dtype)


def _matmul(a, b, out_dtype, transposed_b=False, name="matmul"):
    N, K = b.shape if transposed_b else b.shape[::-1]
    M = a.shape[0]
    tm = _tile(M, (384, 640, 256, 320, 128, 64, 32, 16))
    tn = _tile(N, (1024, 512, 256, 128))
    b_spec = (pl.BlockSpec((tn, K), lambda j, i: (j, 0)) if transposed_b
              else pl.BlockSpec((K, tn), lambda j, i: (0, j)))
    return pl.pallas_call(
        functools.partial(_mm_kernel, transposed_b=transposed_b), grid=(N // tn, M // tm),
        in_specs=[pl.BlockSpec((tm, K), lambda j, i: (i, 0)), b_spec],
        out_specs=pl.BlockSpec((tm, tn), lambda j, i: (i, j)),
        out_shape=jax.ShapeDtypeStruct((M, N), out_dtype),
        compiler_params=_cp(2), name=name)(a, b)


def _rope128(a, cos, sin):
    return a * cos + pltpu.roll(a, ROPE_DIM, 1) * sin


def _post_kernel(cq_ref, ckv_ref, kr_ref, cos_ref, sin_ref, qg_ref, kvg_ref,
                 cqn_ref, kv32_ref, kv16_ref, kr32_ref, kr16_ref):
    cqn_ref[...] = _rmsnorm(cq_ref[...], qg_ref[...]).astype(BF16)
    kvn = _rmsnorm(ckv_ref[...], kvg_ref[...])
    kv32_ref[...] = kvn
    kv16_ref[...] = kvn.astype(BF16)
    r = _rope128(kr_ref[...], cos_ref[...], sin_ref[...])[:, :ROPE_DIM]
    kr32_ref[...] = r
    kr16_ref[...] = r.astype(BF16)


def _post(z, cos, sin, q_norm_g, kv_norm_g, off):
    R = z.shape[0]
    QR, KR = q_norm_g.shape[0], kv_norm_g.shape[0]
    tm = _tile(R, (256, 128, 64, 32, 16))
    return pl.pallas_call(
        _post_kernel, grid=(R // tm,),
        in_specs=[pl.BlockSpec((tm, QR), lambda i: (i, off["cq"] // QR)),
                  pl.BlockSpec((tm, KR), lambda i: (i, off["ckv"] // KR)),
                  pl.BlockSpec((tm, LANES), lambda i: (i, off["kr"] // LANES)),
                  pl.BlockSpec((tm, LANES), lambda i: (i, 0)),
                  pl.BlockSpec((tm, LANES), lambda i: (i, 0)),
                  pl.BlockSpec((1, QR), lambda i: (0, 0)),
                  pl.BlockSpec((1, KR), lambda i: (0, 0))],
        out_specs=[pl.BlockSpec((tm, QR), lambda i: (i, 0)),
                   pl.BlockSpec((tm, KR), lambda i: (i, 0)),
                   pl.BlockSpec((tm, KR), lambda i: (i, 0)),
                   pl.BlockSpec((tm, ROPE_DIM), lambda i: (i, 0)),
                   pl.BlockSpec((tm, ROPE_DIM), lambda i: (i, 0))],
        out_shape=[jax.ShapeDtypeStruct((R, QR), BF16), jax.ShapeDtypeStruct((R, KR), F32),
                   jax.ShapeDtypeStruct((R, KR), BF16), jax.ShapeDtypeStruct((R, ROPE_DIM), F32),
                   jax.ShapeDtypeStruct((R, ROPE_DIM), BF16)],
        compiler_params=_cp(1), name="latent_norm_rope")(
            z, z, z, cos, sin, q_norm_g.reshape(1, QR), kv_norm_g.reshape(1, KR))


def _q_kernel(cqn_ref, w_ref, cos_ref, sin_ref, q_ref):
    acc = _dot(cqn_ref[...], w_ref[0])
    q_ref[0, :, 0:NOPE_DIM] = acc[:, :NOPE_DIM].astype(BF16)
    r = _rope128(acc[:, NOPE_DIM:], cos_ref[...], sin_ref[...])
    q_ref[0, :, NOPE_DIM:NOPE_DIM + ROPE_DIM] = r[:, :ROPE_DIM].astype(BF16)


def _queries(cqn, w_uq3, cos, sin):
    R, QR = cqn.shape
    H = w_uq3.shape[0]
    tm = _tile(R, (768, 512, 256, 128, 64, 32, 16))
    return pl.pallas_call(
        _q_kernel, grid=(R // tm, H),
        in_specs=[pl.BlockSpec((tm, QR), lambda i, h: (i, 0)),
                  pl.BlockSpec((1, QR, NOPE_DIM + 2 * ROPE_DIM), lambda i, h: (h, 0, 0)),
                  pl.BlockSpec((tm, LANES), lambda i, h: (i, 0)),
                  pl.BlockSpec((tm, LANES), lambda i, h: (i, 0))],
        out_specs=pl.BlockSpec((1, tm, NOPE_DIM + ROPE_DIM), lambda i, h: (h, i, 0)),
        out_shape=jax.ShapeDtypeStruct((H, R, NOPE_DIM + ROPE_DIM), BF16),
        compiler_params=_cp(2), name="mla_queries")(cqn, w_uq3, cos, sin)


def _kvx_kernel(kv16_ref, kr16_ref, wuk_ref, wuv_ref, k_ref, v_ref):
    kv = kv16_ref[...]
    k_ref[0, :, 0:NOPE_DIM] = _dot(kv, wuk_ref[0]).astype(BF16)
    k_ref[0, :, NOPE_DIM:NOPE_DIM + ROPE_DIM] = kr16_ref[...]
    v_ref[0] = _dot(kv, wuv_ref[0]).astype(BF16)


def _kv_expand(kv16, kr16, wuk_t, wuv_t):
    R, KR = kv16.shape
    H = wuk_t.shape[0]
    tm = _tile(R, (768, 512, 256, 128, 64, 32, 16))
    return pl.pallas_call(
        _kvx_kernel, grid=(R // tm, H),
        in_specs=[pl.BlockSpec((tm, KR), lambda i, h: (i, 0)),
                  pl.BlockSpec((tm, ROPE_DIM), lambda i, h: (i, 0)),
                  pl.BlockSpec((1, KR, NOPE_DIM), lambda i, h: (h, 0, 0)),
                  pl.BlockSpec((1, KR, V_DIM), lambda i, h: (h, 0, 0))],
        out_specs=[pl.BlockSpec((1, tm, NOPE_DIM + ROPE_DIM), lambda i, h: (h, i, 0)),
                   pl.BlockSpec((1, tm, V_DIM), lambda i, h: (h, i, 0))],
        out_shape=[jax.ShapeDtypeStruct((H, R, NOPE_DIM + ROPE_DIM), BF16),
                   jax.ShapeDtypeStruct((H, R, V_DIM), BF16)],
        compiler_params=_cp(2), name="mla_kv_expand")(kv16, kr16, wuk_t, wuv_t)


def _attn_kernel(q_ref, k_ref, v_ref, km_ref, vm_ref, o_ref, *, tq, tk):
    qi = pl.program_id(2)
    q = q_ref[0]
    s0 = _dot_nt(q, km_ref[0]) * ATTN_SCALE
    m = jnp.max(s0, axis=-1, keepdims=True)
    p0 = jnp.exp(s0 - m)
    l = jnp.sum(p0, axis=-1, keepdims=True)
    acc = _dot(p0.astype(BF16), vm_ref[0])

    def update(carry, s, v):
        m, l, acc = carry
        m_new = jnp.maximum(m, jnp.max(s, axis=-1, keepdims=True))
        p = jnp.exp(s - m_new)
        corr = jnp.exp(m - m_new)
        return m_new, l * corr + jnp.sum(p, axis=-1, keepdims=True), acc * corr + _dot(p.astype(BF16), v)

    def below_diagonal(j, carry):
        start = pl.multiple_of(j * tk, tk)
        s = _dot_nt(q, k_ref[0, pl.ds(start, tk), :]) * ATTN_SCALE
        return update(carry, s, v_ref[0, pl.ds(start, tk), :])

    carry = lax.fori_loop(0, qi * (tq // tk), below_diagonal, (m, l, acc))
    start = pl.multiple_of(qi * tq, tq)
    s = _dot_nt(q, k_ref[0, pl.ds(start, tq), :]) * ATTN_SCALE
    row = lax.broadcasted_iota(jnp.int32, (tq, tq), 0)
    col = lax.broadcasted_iota(jnp.int32, (tq, tq), 1)
    m, l, acc = update(carry, jnp.where(col <= row, s, -jnp.inf), v_ref[0, pl.ds(start, tq), :])
    o_ref[...] = (acc / l).astype(o_ref.dtype)


def _prompt_attention(Q, K, V, B, SEQ, out_rows, meta_row0):
    H = Q.shape[0]
    tq = _tile(SEQ, (512, 256, 128))
    tk = tq
    nq = SEQ // tq
    mb = meta_row0 // N_META
    dk = NOPE_DIM + ROPE_DIM
    return pl.pallas_call(
        functools.partial(_attn_kernel, tq=tq, tk=tk), grid=(B, H, nq),
        in_specs=[pl.BlockSpec((1, tq, dk), lambda b, h, i: (h, b * nq + i, 0)),
                  pl.BlockSpec((1, SEQ, dk), lambda b, h, i: (h, b, 0)),
                  pl.BlockSpec((1, SEQ, V_DIM), lambda b, h, i: (h, b, 0)),
                  pl.BlockSpec((1, N_META, dk), lambda b, h, i: (h, mb + b, 0)),
                  pl.BlockSpec((1, N_META, V_DIM), lambda b, h, i: (h, mb + b, 0))],
        out_specs=pl.BlockSpec((tq, V_DIM), lambda b, h, i: (b * nq + i, h)),
        out_shape=jax.ShapeDtypeStruct((out_rows, H * V_DIM), BF16),
        compiler_params=_cp(3), name="mla_prompt_attention")(Q, K, V, K, V)


def _qlat_kernel(q_ref, wuk_ref, o_ref):
    q = q_ref[0]
    kr = wuk_ref.shape[1]
    o_ref[0, :, 0:kr] = _dot_nt(q[:, :NOPE_DIM], wuk_ref[0]).astype(BF16)
    o_ref[0, :, kr:kr + ROPE_DIM] = q[:, NOPE_DIM:]


def _q_latent(Q, wuk_t, row_blk, DB):
    H, _, KR = wuk_t.shape[0], None, wuk_t.shape[1]
    dk = NOPE_DIM + ROPE_DIM
    return pl.pallas_call(
        _qlat_kernel, grid=(H,),
        in_specs=[pl.BlockSpec((1, DB, dk), lambda h: (h, row_blk, 0)),
                  pl.BlockSpec((1, KR, NOPE_DIM), lambda h: (h, 0, 0))],
        out_specs=pl.BlockSpec((1, DB, KR + ROPE_DIM), lambda h: (h, 0, 0)),
        out_shape=jax.ShapeDtypeStruct((H, DB, KR + ROPE_DIM), BF16),
        compiler_params=_cp(1), name="mla_q_latent")(Q, wuk_t)


def _paged_kernel(pt_ref, q_ref, kvs_ref, krs_ref, kv_hbm, krt_hbm, o_ref,
                  m_s, l_s, acc_s, kv_in, krt_in, kv_buf, krt_buf, sem, *, npg, kr):
    b = pl.program_id(0)
    j = pl.program_id(1)
    nj = pl.num_programs(1)
    step = b * nj + j
    slot = step % 2
    page = kv_in.shape[2]

    def page_copies(bb, jj, sl):
        copies = []
        for p in range(npg):
            pg = pt_ref[bb, jj * npg + p]
            copies.append(pltpu.make_async_copy(kv_hbm.at[pg], kv_in.at[sl, p], sem.at[sl, 0]))
            copies.append(pltpu.make_async_copy(krt_hbm.at[pg], krt_in.at[sl, p], sem.at[sl, 1]))
        return copies

    @pl.when(step == 0)
    def _():
        for c in page_copies(0, 0, 0):
            c.start()

    @pl.when(step + 1 < pl.num_programs(0) * nj)
    def _():
        wrap = j + 1 == nj
        for c in page_copies(jnp.where(wrap, b + 1, b), jnp.where(wrap, 0, j + 1), 1 - slot):
            c.start()

    for c in page_copies(b, j, slot):
        c.wait()

    @pl.when(j == 0)
    def _():
        m_s[...] = jnp.full(m_s.shape, -jnp.inf, F32)
        l_s[...] = jnp.zeros(l_s.shape, F32)
        acc_s[...] = jnp.zeros(acc_s.shape, F32)

    for p in range(npg):
        kv_buf[p * page:(p + 1) * page, :] = kv_in[slot, p].astype(BF16)
        krt_buf[:, p * page:(p + 1) * page] = krt_in[slot, p].astype(BF16)
    q = q_ref[0]
    ql = q[:, :kr]
    qr = q[:, kr:]
    kv16 = kv_buf[...]
    s = (_dot_nt(ql, kv16) + _dot(qr, krt_buf[...])) * ATTN_SCALE
    m_prev = m_s[...]
    m_new = jnp.maximum(m_prev, jnp.max(s, axis=-1, keepdims=True))
    corr = jnp.exp(m_prev - m_new)
    p = jnp.exp(s - m_new)
    l = l_s[...] * corr + jnp.sum(p, axis=-1, keepdims=True)
    acc = acc_s[...] * corr + _dot(p.astype(BF16), kv16)
    m_s[...] = m_new
    l_s[...] = l
    acc_s[...] = acc

    @pl.when(j == pl.num_programs(1) - 1)
    def _():
        kvs = kvs_ref[0].astype(F32)
        krs = krs_ref[0].astype(F32)
        s = (jnp.sum(ql.astype(F32) * kvs, axis=-1, keepdims=True)
             + jnp.sum(qr.astype(F32) * krs, axis=-1, keepdims=True)) * ATTN_SCALE
        m2 = jnp.maximum(m_new, s)
        p = jnp.exp(s - m2)
        c2 = jnp.exp(m_new - m2)
        o_ref[0] = ((acc * c2 + p * kvs) / (l * c2 + p)).astype(o_ref.dtype)


def _paged_attention(page_table, Qs, cache_kv, cache_krt, kvs, krs):
    DB, H, dq = Qs.shape
    _, PAGE, KR = cache_kv.shape
    NP = page_table.shape[1]
    npg = _tile(NP, (PAGES_PER_STEP, 16, 8, 4, 2, 1))

    in_specs = [pl.BlockSpec((1, H, dq), lambda b, j, pt: (b, 0, 0)),
                pl.BlockSpec((1, 1, KR), lambda b, j, pt: (b, 0, 0)),
                pl.BlockSpec((1, 1, ROPE_DIM), lambda b, j, pt: (b, 0, 0)),
                pl.BlockSpec(memory_space=pl.ANY), pl.BlockSpec(memory_space=pl.ANY)]
    return pl.pallas_call(
        functools.partial(_paged_kernel, npg=npg, kr=KR),
        grid_spec=pltpu.PrefetchScalarGridSpec(
            num_scalar_prefetch=1, grid=(DB, NP // npg), in_specs=in_specs,
            out_specs=pl.BlockSpec((1, H, KR), lambda b, j, pt: (b, 0, 0)),
            scratch_shapes=[pltpu.VMEM((H, 1), F32), pltpu.VMEM((H, 1), F32), pltpu.VMEM((H, KR), F32),
                            pltpu.VMEM((2, npg, PAGE, KR), F32), pltpu.VMEM((2, npg, ROPE_DIM, PAGE), F32),
                            pltpu.VMEM((npg * PAGE, KR), BF16), pltpu.VMEM((ROPE_DIM, npg * PAGE), BF16),
                            pltpu.SemaphoreType.DMA((2, 2))]),
        out_shape=jax.ShapeDtypeStruct((DB, H, KR), BF16),
        compiler_params=_cp(2), name="mla_paged_attention")(
            page_table, Qs, kvs, krs, cache_kv, cache_krt)


def _ov_kernel(a_ref, w_ref, o_ref):
    o_ref[...] = _dot(a_ref[0], w_ref[0]).astype(o_ref.dtype)


def _sample_values(acc_t, wuv_t):
    H, DB, KR = acc_t.shape
    return pl.pallas_call(
        _ov_kernel, grid=(H,),
        in_specs=[pl.BlockSpec((1, DB, KR), lambda h: (h, 0, 0)),
                  pl.BlockSpec((1, KR, V_DIM), lambda h: (h, 0, 0))],
        out_specs=pl.BlockSpec((DB, V_DIM), lambda h: (0, h)),
        out_shape=jax.ShapeDtypeStruct((DB, H * V_DIM), BF16),
        compiler_params=_cp(1), name="mla_sample_values")(acc_t, wuv_t)


def _mlstm_kernel(bif_ref, q_ref, k_ref, v_ref, og_ref, gc_ref, gi_ref, gf_ref, mng_ref, c0_ref, n0_ref, m0_ref,
                  om_ref, co_ref, no_ref, mo_ref, c_s, n_s, m_s, *, L, nc, dqk):
    h = pl.program_id(1)
    c = pl.program_id(2)

    @pl.when(c == 0)
    def _():
        c_s[...] = c0_ref[0, 0]
        n_s[...] = n0_ref[0, 0]
        m_s[...] = m0_ref[0, 0]

    b_i = bif_ref[h]
    b_f = bif_ref[H_M + h]
    gc = gc_ref[...]
    lane = lax.broadcasted_iota(jnp.int32, gc.shape, 1)
    li_c = jnp.sum(jnp.where(lane == h, gc, 0.0), axis=1, keepdims=True) + b_i
    lf_c = _log_sigmoid(jnp.sum(jnp.where(lane == H_M + h, gc, 0.0), axis=1, keepdims=True) + b_f)
    li_r = gi_ref[0, 0] + b_i
    lf_r = _log_sigmoid(gf_ref[0, 0] + b_f)

    row = lax.broadcasted_iota(jnp.int32, (L, L), 0)
    col = lax.broadcasted_iota(jnp.int32, (L, L), 1)
    tri = col <= row
    f_c = jnp.sum(jnp.where(tri, lf_r, 0.0), axis=1, keepdims=True)
    f_r = jnp.sum(jnp.where(row <= col, lf_c, 0.0), axis=0, keepdims=True)
    m_prev = m_s[...]
    dmat = jnp.where(tri, f_c - f_r + li_r, -jnp.inf)
    b = f_c + m_prev
    m_row = jnp.maximum(b, jnp.max(dmat, axis=1, keepdims=True))

    qs = q_ref[...] * (dqk ** -0.5)
    k = k_ref[...]
    v = v_ref[...]
    q16 = qs.astype(BF16)
    k16 = k.astype(BF16)
    w = _dot_nt(q16, k16) * jnp.exp(dmat - m_row)
    bw = jnp.exp(b - m_row)
    c_old = c_s[...]
    n_old = n_s[...]
    num = bw * _dot_nt(q16, c_old.astype(BF16)) + _dot(w.astype(BF16), v.astype(BF16))
    den = bw * jnp.sum(n_old * qs, axis=1, keepdims=True) + jnp.sum(w, axis=1, keepdims=True)
    hc = num / jnp.maximum(jnp.abs(den), jnp.exp(-m_row))

    m_new = m_row[L - 1:L, :]
    f_last = f_c[L - 1:L, :]
    w_s = jnp.exp(f_last - f_c + li_c - m_new)
    decay = jnp.exp(f_last + m_prev - m_new)
    c_s[...] = decay * c_old + _dot_tn((v * w_s).astype(BF16), k16)
    n_s[...] = decay * n_old + jnp.sum(w_s * k, axis=0, keepdims=True)
    m_s[...] = m_new

    hn = hc * lax.rsqrt(jnp.mean(hc * hc, axis=-1, keepdims=True) + EPS) * mng_ref[...]
    om_ref[...] = (hn * _sigmoid(og_ref[...])).astype(om_ref.dtype)

    @pl.when(c == nc - 1)
    def _():
        co_ref[0, 0] = c_s[...]
        no_ref[0, 0] = n_s[...]
        mo_ref[0, 0] = m_s[...]


def _mlstm_chunks(z, g_rows, b_if, m_norm_g, c0, n0, m0, off, *, B, L, nc, row0, out_rows, dqk, dv):
    rb0 = row0 // L

    def rb(b, c):
        return rb0 + b * nc + c

    state = lambda shp: pl.BlockSpec((1, 1) + shp, lambda b, h, c: (b, h, 0, 0))
    return pl.pallas_call(
        functools.partial(_mlstm_kernel, L=L, nc=nc, dqk=dqk), grid=(B, H_M, nc),
        in_specs=[pl.BlockSpec(memory_space=pltpu.SMEM),
                  pl.BlockSpec((L, dqk), lambda b, h, c: (rb(b, c), off["mq"] // dqk + h)),
                  pl.BlockSpec((L, dqk), lambda b, h, c: (rb(b, c), off["mk"] // dqk + h)),
                  pl.BlockSpec((L, dv), lambda b, h, c: (rb(b, c), off["mv"] // dv + h)),
                  pl.BlockSpec((L, dv), lambda b, h, c: (rb(b, c), off["mo"] // dv + h)),
                  pl.BlockSpec((L, LANES), lambda b, h, c: (rb(b, c), off["gates"] // LANES)),
                  pl.BlockSpec((1, 1, 1, L), lambda b, h, c: (rb(b, c), h, 0, 0)),
                  pl.BlockSpec((1, 1, 1, L), lambda b, h, c: (rb(b, c), H_M + h, 0, 0)),
                  pl.BlockSpec((1, dv), lambda b, h, c: (0, h)),
                  state((dv, dqk)), state((1, dqk)), state((1, 1))],
        out_specs=[pl.BlockSpec((L, dv), lambda b, h, c: (b * nc + c, h)),
                   state((dv, dqk)), state((1, dqk)), state((1, 1))],
        out_shape=[jax.ShapeDtypeStruct((out_rows, H_M * dv), BF16),
                   jax.ShapeDtypeStruct((B, H_M, dv, dqk), F32),
                   jax.ShapeDtypeStruct((B, H_M, 1, dqk), F32),
                   jax.ShapeDtypeStruct((B, H_M, 1, 1), F32)],
        scratch_shapes=[pltpu.VMEM((dv, dqk), F32), pltpu.VMEM((1, dqk), F32), pltpu.VMEM((1, 1), F32)],
        compiler_params=_cp(3), name=f"mlstm_chunks_L{L}")(
            b_if, z, z, z, z, z, g_rows, g_rows, m_norm_g.reshape(1, H_M * dv), c0, n0, m0)


def _mlstm_step_kernel(q_ref, k_ref, v_ref, og_ref, g_ref, bif_ref, mng_ref, c_ref, n_ref, m_ref,
                       om_ref, co_ref, no_ref, mo_ref, *, dqk):
    g = g_ref[0] + bif_ref[...]
    li = g[0:H_M]
    lf = _log_sigmoid(g[H_M:2 * H_M])
    m_prev = m_ref[0]
    qs = q_ref[0] * (dqk ** -0.5)
    k = k_ref[0]
    v = v_ref[0]
    n_old = n_ref[0]
    b = lf + m_prev
    m_row = jnp.maximum(b, li)
    q16 = qs.astype(BF16)
    k16 = k.astype(BF16)
    v16 = v.astype(BF16)
    w = jnp.sum(q16.astype(F32) * k16.astype(F32), axis=1, keepdims=True) * jnp.exp(li - m_row)
    bw = jnp.exp(b - m_row)
    den = bw * jnp.sum(n_old * qs, axis=1, keepdims=True) + w
    dn = jnp.maximum(jnp.abs(den), jnp.exp(-m_row))
    w_s = jnp.exp(li - m_row)
    decay = jnp.exp(lf + m_prev - m_row)
    no_ref[0] = decay * n_old + w_s * k
    mo_ref[0] = m_row
    vw = v * w_s
    zeros_v = jnp.zeros_like(vw)
    k_pad = jnp.concatenate([k, jnp.zeros_like(k)], axis=0).astype(BF16)
    hrow = lax.broadcasted_iota(jnp.int32, vw.shape, 0)
    outs = []
    for h in range(H_M):
        c_old = c_ref[0, h]
        cq = _dot_nt(q16, c_old.astype(BF16))[h:h + 1]
        num = bw[h:h + 1] * cq + w[h:h + 1] * v16[h:h + 1].astype(F32)
        outs.append(num / dn[h:h + 1])
        vw_h = jnp.concatenate([jnp.where(hrow == h, vw, 0.0), zeros_v], axis=0).astype(BF16)
        co_ref[0, h] = decay[h:h + 1] * c_old + _dot_tn(vw_h, k_pad)
    hc = jnp.concatenate(outs, axis=0)
    hn = hc * lax.rsqrt(jnp.mean(hc * hc, axis=-1, keepdims=True) + EPS) * mng_ref[...]
    om_ref[0] = hn * _sigmoid(og_ref[0])


def _mlstm_step(q, k, v, og, g, b_if, m_norm_g, c, n, m):
    DB, _, dqk = q.shape
    dv = v.shape[2]
    per_b = lambda *shp: pl.BlockSpec((1,) + shp, lambda b: (b,) + (0,) * len(shp))
    full = lambda *shp: pl.BlockSpec(shp, lambda b: (0,) * len(shp))
    return pl.pallas_call(
        functools.partial(_mlstm_step_kernel, dqk=dqk), grid=(DB,),
        in_specs=[per_b(H_M, dqk), per_b(H_M, dqk), per_b(H_M, dv), per_b(H_M, dv), per_b(2 * H_M, 1),
                  full(2 * H_M, 1), full(H_M, dv), per_b(H_M, dv, dqk), per_b(H_M, dqk), per_b(H_M, 1)],
        out_specs=[per_b(H_M, dv), per_b(H_M, dv, dqk), per_b(H_M, dqk), per_b(H_M, 1)],
        out_shape=[jax.ShapeDtypeStruct((DB, H_M, dv), F32), jax.ShapeDtypeStruct((DB, H_M, dv, dqk), F32),
                   jax.ShapeDtypeStruct((DB, H_M, dqk), F32), jax.ShapeDtypeStruct((DB, H_M, 1), F32)],
        compiler_params=_cp(1), name="mlstm_step")(
            q, k, v, og, g, b_if.reshape(2 * H_M, 1), m_norm_g.reshape(H_M, dv), c, n, m)


def _merge_kernel(oa_ref, om_ref, wa_ref, wm_ref, ga_ref, gm_ref, o_ref):
    a = _dot(oa_ref[...], wa_ref[...])
    m = _dot(om_ref[...], wm_ref[...])
    o_ref[...] = (_sigmoid(ga_ref[...]) * a + _sigmoid(gm_ref[...]) * m).astype(o_ref.dtype)


def _merge(o_a, o_m, w_br_a, w_br_m, z, off):
    M, Ka = o_a.shape
    Km = o_m.shape[1]
    N = w_br_a.shape[1]
    tm = _tile(M, (640, 320, 256, 128, 64, 16))
    tn = _tile(N, (512, 256, 128))
    return pl.pallas_call(
        _merge_kernel, grid=(M // tm, N // tn),
        in_specs=[pl.BlockSpec((tm, Ka), lambda i, j: (i, 0)),
                  pl.BlockSpec((tm, Km), lambda i, j: (i, 0)),
                  pl.BlockSpec((Ka, tn), lambda i, j: (0, j)),
                  pl.BlockSpec((Km, tn), lambda i, j: (0, j)),
                  pl.BlockSpec((tm, tn), lambda i, j: (i, off["ga"] // tn + j)),
                  pl.BlockSpec((tm, tn), lambda i, j: (i, off["gm"] // tn + j))],
        out_specs=pl.BlockSpec((tm, tn), lambda i, j: (i, j)),
        out_shape=jax.ShapeDtypeStruct((M, N), BF16),
        compiler_params=_cp(2), name="branch_merge")(o_a, o_m, w_br_a, w_br_m, z, z)


SLAB = 8


def _store_slabs(o_ref, y):
    w = o_ref.shape[2]
    for i in range(SLAB):
        o_ref[:, i, :] = y[:, i * w:(i + 1) * w]


def _load_slabs(x_ref):
    return jnp.concatenate([x_ref[:, i, :] for i in range(SLAB)], axis=1)


def _ln_router_kernel(h_ref, mix_ref, g_ref, b_ref, wr_ref, br_ref, o32_ref, ei_ref, pr_ref, *, n_exp):
    y = _layernorm(ALPHA * h_ref[...] + mix_ref[...], g_ref[...], b_ref[...])
    _store_slabs(o32_ref, y)
    logits = _dot(y.astype(BF16), wr_ref[...]) + br_ref[...]
    lane = lax.broadcasted_iota(jnp.int32, logits.shape, 1)
    logits = jnp.where(lane < n_exp, logits, -jnp.inf)
    vals, idxs = [], []
    for _ in range(TOP_K):
        mx = jnp.max(logits, axis=1, keepdims=True)
        ix = jnp.min(jnp.where(logits == mx, lane, LANES), axis=1, keepdims=True)
        vals.append(mx)
        idxs.append(ix)
        logits = jnp.where(lane == ix, -jnp.inf, logits)
    es = [jnp.exp(vv - vals[0]) for vv in vals]
    tot = es[0]
    for e in es[1:]:
        tot = tot + e
    ei = jnp.zeros(lane.shape, jnp.int32)
    pr = jnp.zeros(lane.shape, F32)
    for kk in range(TOP_K):
        ei = jnp.where(lane == kk, idxs[kk], ei)
        pr = jnp.where(lane == kk, es[kk] / tot, pr)
    ei_ref[...] = ei
    pr_ref[...] = pr


def _ln_router(h32, mix, g, b, w_router, b_router):
    M, D = mix.shape
    E = w_router.shape[1]
    wr = jnp.zeros((D, LANES), BF16).at[:, :E].set(w_router.astype(BF16))
    br = jnp.zeros((1, LANES), F32).at[0, :E].set(b_router)
    tm = _tile(M, (128, 64, 32, 16, 8))
    row = pl.BlockSpec((tm, D), lambda i: (i, 0))
    vec = pl.BlockSpec((1, D), lambda i: (0, 0))
    nar = pl.BlockSpec((tm, LANES), lambda i: (i, 0))
    return pl.pallas_call(
        functools.partial(_ln_router_kernel, n_exp=E), grid=(M // tm,),
        in_specs=[row, row, vec, vec, pl.BlockSpec((D, LANES), lambda i: (0, 0)),
                  pl.BlockSpec((1, LANES), lambda i: (0, 0))],
        out_specs=[pl.BlockSpec((tm, SLAB, D // SLAB), lambda i: (i, 0, 0)), nar, nar],
        out_shape=[jax.ShapeDtypeStruct((M, SLAB, D // SLAB), F32),
                   jax.ShapeDtypeStruct((M, LANES), jnp.int32), jax.ShapeDtypeStruct((M, LANES), F32)],
        compiler_params=_cp(1), name="ln1_router")(h32, mix, g.reshape(1, D), b.reshape(1, D), wr, br)


def _gather_kernel(valid_ref, tok_ref, x_ref, o_ref, buf, sem):
    i = pl.program_id(0)

    @pl.when(valid_ref[i] > 0)
    def _():
        def copy(r):
            return pltpu.make_async_copy(x_ref.at[tok_ref[0, 0, r]], buf.at[r], sem)

        def start(r, _):
            copy(r).start()
            return 0

        def wait(r, _):
            copy(r).wait()
            return 0

        lax.fori_loop(0, MOE_SUB, start, 0, unroll=8)
        lax.fori_loop(0, MOE_SUB, wait, 0, unroll=8)
        o_ref[...] = _load_slabs(buf).astype(o_ref.dtype)

    @pl.when(valid_ref[i] == 0)
    def _():
        o_ref[...] = jnp.zeros(o_ref.shape, o_ref.dtype)


def _dispatch(x3, slot_tok, sub_valid, n_slots):
    D = x3.shape[1] * x3.shape[2]
    nsub = n_slots // MOE_SUB
    return pl.pallas_call(
        _gather_kernel,
        grid_spec=pltpu.PrefetchScalarGridSpec(
            num_scalar_prefetch=1, grid=(nsub,),
            in_specs=[pl.BlockSpec((1, 1, MOE_SUB), lambda i, v: (i, 0, 0), memory_space=pltpu.SMEM),
                      pl.BlockSpec(memory_space=pl.ANY)],
            out_specs=pl.BlockSpec((MOE_SUB, D), lambda i, v: (i, 0)),
            scratch_shapes=[pltpu.VMEM((MOE_SUB,) + x3.shape[1:], F32), pltpu.SemaphoreType.DMA(())]),
        out_shape=jax.ShapeDtypeStruct((n_slots, D), BF16),
        compiler_params=_cp(1), name="moe_dispatch")(
            sub_valid, slot_tok.reshape(nsub, 1, MOE_SUB), x3)


def _per_super_tile(nv, o_ref, compute):
    @pl.when(nv > 1)
    def _():
        compute(MOE_SUPER)

    @pl.when(nv == 1)
    def _():
        compute(MOE_SUB)
        o_ref[MOE_SUB:, :] = jnp.zeros((MOE_SUPER - MOE_SUB, o_ref.shape[1]), o_ref.dtype)

    @pl.when(nv == 0)
    def _():
        o_ref[...] = jnp.zeros(o_ref.shape, o_ref.dtype)


def _ffn_up_kernel(se_ref, sb_ref, nv_ref, x_ref, wg_ref, wu_ref, bg_ref, bu_ref, o_ref):
    def compute(rows):
        x = x_ref[0:rows, :]
        g = jnp.minimum(_dot(x, wg_ref[0].astype(BF16)) + bg_ref[0], SWIGLU_LIMIT)
        u = jnp.clip(_dot(x, wu_ref[0].astype(BF16)) + bu_ref[0], -SWIGLU_LIMIT, SWIGLU_LIMIT)
        o_ref[0:rows, :] = (g * _sigmoid(SWIGLU_ALPHA * g) * (u + 1.0)).astype(o_ref.dtype)

    _per_super_tile(nv_ref[pl.program_id(0)], o_ref, compute)


def _weight_chunk(n_chunks):
    def f(s, c, se, sb, nv):
        return (se[s], 0, jnp.where(nv[s] > 0, c, n_chunks - 1))
    return f


def _ffn_up(st_e, st_blk, st_nv, xs, w_gate, w_up, b_gate, b_up, n_super):
    E, D, F = w_gate.shape
    tf = _tile(F, (256, 128))
    w_spec = pl.BlockSpec((1, D, tf), _weight_chunk(F // tf))
    b_spec = pl.BlockSpec((1, 1, tf), _weight_chunk(F // tf))
    return pl.pallas_call(
        _ffn_up_kernel,
        grid_spec=pltpu.PrefetchScalarGridSpec(
            num_scalar_prefetch=3, grid=(n_super, F // tf),
            in_specs=[pl.BlockSpec((MOE_SUPER, D), lambda s, c, se, sb, nv: (sb[s], 0)),
                      w_spec, w_spec, b_spec, b_spec],
            out_specs=pl.BlockSpec((MOE_SUPER, tf), lambda s, c, se, sb, nv: (s, c))),
        out_shape=jax.ShapeDtypeStruct((n_super * MOE_SUPER, F), BF16),
        compiler_params=_cp(2), name="moe_ffn_up")(
            st_e, st_blk, st_nv, xs, w_gate, w_up, b_gate.reshape(E, 1, F), b_up.reshape(E, 1, F))


def _ffn_down_kernel(se_ref, sb_ref, nv_ref, a_ref, wd_ref, bd_ref, o_ref):
    def compute(rows):
        o_ref[0:rows, :] = _dot(a_ref[0:rows, :], wd_ref[0].astype(BF16)) + bd_ref[0]

    _per_super_tile(nv_ref[pl.program_id(0)], o_ref, compute)


def _ffn_down(st_e, st_blk, st_nv, act, w_down, b_down, n_super):
    E, F, D = w_down.shape
    tn = _tile(D, (256, 128))
    return pl.pallas_call(
        _ffn_down_kernel,
        grid_spec=pltpu.PrefetchScalarGridSpec(
            num_scalar_prefetch=3, grid=(n_super, D // tn),
            in_specs=[pl.BlockSpec((MOE_SUPER, F), lambda s, c, se, sb, nv: (sb[s], 0)),
                      pl.BlockSpec((1, F, tn), _weight_chunk(D // tn)),
                      pl.BlockSpec((1, 1, tn), _weight_chunk(D // tn))],
            out_specs=pl.BlockSpec((MOE_SUPER, tn), lambda s, c, se, sb, nv: (s, c))),
        out_shape=jax.ShapeDtypeStruct((n_super * MOE_SUPER, D), F32),
        compiler_params=_cp(2), name="moe_ffn_down")(
            st_e, st_blk, st_nv, act, w_down, b_down.reshape(E, 1, D))


def _combine_kernel(slot_ref, ys_ref, pr_ref, h_ref, g_ref, b_ref, o_ref, buf, sem, *, tt):
    def copy(r):
        k = r // tt
        t = r - k * tt
        return pltpu.make_async_copy(ys_ref.at[pl.ds(slot_ref[0, 0, r], 1), :],
                                     buf.at[k, pl.ds(t, 1), :], sem)

    def start(r, _):
        copy(r).start()
        return 0

    def wait(r, _):
        copy(r).wait()
        return 0

    lax.fori_loop(0, TOP_K * tt, start, 0, unroll=8)
    lax.fori_loop(0, TOP_K * tt, wait, 0, unroll=8)
    pr = pr_ref[...]
    y = pr[:, 0:1] * buf[0]
    for k in range(1, TOP_K):
        y = y + pr[:, k:k + 1] * buf[k]
    o_ref[...] = _layernorm(ALPHA * _load_slabs(h_ref) + y, g_ref[...], b_ref[...])


def _combine(slots, ys, probs, h3, g, b):
    M = h3.shape[0]
    D = h3.shape[1] * h3.shape[2]
    tt = _tile(M, (64, 32, 16, 8))
    row = pl.BlockSpec((tt, D), lambda i: (i, 0))
    vec = pl.BlockSpec((1, D), lambda i: (0, 0))
    return pl.pallas_call(
        functools.partial(_combine_kernel, tt=tt), grid=(M // tt,),
        in_specs=[pl.BlockSpec((1, 1, TOP_K * tt), lambda i: (i, 0, 0), memory_space=pltpu.SMEM),
                  pl.BlockSpec(memory_space=pl.ANY),
                  pl.BlockSpec((tt, LANES), lambda i: (i, 0)),
                  pl.BlockSpec((tt,) + h3.shape[1:], lambda i: (i, 0, 0)), vec, vec],
        out_specs=row,
        out_shape=jax.ShapeDtypeStruct((M, D), F32),
        scratch_shapes=[pltpu.VMEM((TOP_K, tt, D), F32), pltpu.SemaphoreType.DMA(())],
        compiler_params=_cp(1), name="moe_combine_ln2")(
            slots, ys, probs, h3, g.reshape(1, D), b.reshape(1, D))


def _moe_plan(eid, n_exp, n_super):
    M = eid.shape[0]
    S = M * TOP_K
    ef = eid.reshape(S)
    onehot = (ef[:, None] == jnp.arange(n_exp, dtype=jnp.int32)[None, :]).astype(jnp.int32)
    seen = jnp.cumsum(onehot, axis=0)
    cnt = seen[-1]
    rank = jnp.sum((seen - onehot) * onehot, axis=1)
    nsup = (cnt + MOE_SUPER - 1) // MOE_SUPER
    sup_end = jnp.cumsum(nsup)
    sup_start = sup_end - nsup
    slot_of = sup_start[ef] * MOE_SUPER + rank
    n_slots = n_super * MOE_SUPER
    slot_tok = jnp.zeros((n_slots,), jnp.int32).at[slot_of].set(jnp.arange(S, dtype=jnp.int32) // TOP_K)
    total = sup_end[-1]
    sidx = jnp.arange(n_super, dtype=jnp.int32)
    used = sidx < total
    s_eff = jnp.minimum(sidx, total - 1)
    st_e = jnp.minimum(jnp.searchsorted(sup_end, s_eff, side="right"), n_exp - 1).astype(jnp.int32)
    rows = jnp.where(used, jnp.clip(cnt[st_e] - (s_eff - sup_start[st_e]) * MOE_SUPER, 0, MOE_SUPER), 0)
    sub_valid = (rows[:, None] > jnp.arange(MOE_NSUB, dtype=jnp.int32)[None, :] * MOE_SUB)
    st_nv = ((rows + MOE_SUB - 1) // MOE_SUB).astype(jnp.int32)
    return (slot_tok, slot_of, st_e, s_eff.astype(jnp.int32), st_nv,
            sub_valid.astype(jnp.int32).reshape(n_super * MOE_NSUB))


def kernel(x_prompt, x_sample, cache_kv, cache_kr, state_C, state_n, state_m, page_table, meta, ln_in_g, ln_in_b,
           w_in, b_if, q_norm_g, w_uq, kv_norm_g, w_uk, w_uv, m_norm_g, w_br_a, w_br_m, w_out, ln1_g, ln1_b,
           w_router, b_router, w_gate, b_gate, w_up, b_up, w_down, b_down, ln2_g, ln2_b):
    B, SEQ, D = x_prompt.shape
    DB, DS, _ = x_sample.shape
    assert DS == 1 and w_in.shape[0] == DEPTH == 1
    NP = page_table.shape[1]
    PAGE = cache_kv.shape[2]
    QR, KR = q_norm_g.shape[1], kv_norm_g.shape[1]
    E = w_router.shape[2]
    dqk, dv = D // 2 // H_M, D // H_M
    T = B * SEQ
    M = T + DB
    meta0 = M
    R = -(-(M + B * N_META) // ROW_ALIGN) * ROW_ALIGN
    assert T % DB == 0 and M % N_META == 0 and SEQ % CHUNK == 0

    wt = jnp.swapaxes(w_in[0], 0, 1)
    sizes = (QR, KR, ROPE_DIM, H_M * dqk, H_M * dqk, H_M * dv, H_M * dv, H_M, H_M, D, D)
    starts = np.concatenate([[0], np.cumsum(sizes)]).tolist()
    seg = {n: wt[starts[i]:starts[i + 1]] for i, n in enumerate(
        ("cq", "ckv", "kr", "mq", "mk", "mv", "mo", "mi", "mf", "ga", "gm"))}

    def half_swap(wr, axis):
        lo, hi = jnp.split(wr, 2, axis=axis)
        return jnp.concatenate([-hi, lo], axis=axis)

    groups = [("mq", seg["mq"]), ("mk", seg["mk"]), ("mv", seg["mv"]), ("mo", seg["mo"]), ("ga", seg["ga"]),
              ("gm", seg["gm"]), ("cq", seg["cq"]), ("ckv", seg["ckv"]),
              ("kr", jnp.concatenate([seg["kr"], half_swap(seg["kr"], 0)], axis=0)),
              ("gates", jnp.concatenate([seg["mi"], seg["mf"], jnp.zeros((LANES - 2 * H_M, D), F32)], axis=0))]
    off, pos = {}, 0
    for n, g in groups:
        off[n] = pos
        pos += g.shape[0]
    n2 = -(-pos // 1024) * 1024
    w2t = jnp.concatenate([g for _, g in groups] + [jnp.zeros((n2 - pos, D), F32)], axis=0).astype(BF16)
    for n, blk in (("mq", dqk), ("mk", dqk), ("mv", dv), ("mo", dv), ("cq", QR), ("ckv", KR), ("kr", LANES),
                   ("gates", LANES), ("ga", 512), ("gm", 512)):
        assert off[n] % blk == 0, (n, off[n], blk)

    dk = NOPE_DIM + ROPE_DIM
    wq = w_uq[0].reshape(QR, H_A, dk)
    wq_r = wq[..., NOPE_DIM:]
    w_uq3 = jnp.transpose(jnp.concatenate([wq, half_swap(wq_r, 2)], axis=-1), (1, 0, 2)).astype(BF16)
    wuk_t = jnp.transpose(w_uk[0], (1, 0, 2)).astype(BF16)
    wuv_t = jnp.transpose(w_uv[0], (1, 0, 2)).astype(BF16)

    pos_rows = np.zeros((R,), np.float32)
    pos_rows[:T] = np.tile(N_META + np.arange(SEQ), B)
    pos_rows[T:M] = NP * PAGE
    pos_rows[meta0:meta0 + B * N_META] = np.tile(np.arange(N_META), B)
    inv = 1.0 / (ROPE_THETA ** (jnp.arange(0, ROPE_DIM, 2, dtype=F32) / ROPE_DIM))
    ang = jnp.asarray(pos_rows)[:, None] * inv
    cos_t = jnp.tile(jnp.cos(ang), (1, LANES // (ROPE_DIM // 2)))
    sin_t = jnp.tile(jnp.sin(ang), (1, LANES // (ROPE_DIM // 2)))

    x_rest = jnp.concatenate([x_sample.reshape(DB, D), jnp.tile(meta.astype(F32), (B, 1)),
                              jnp.zeros((R - M - B * N_META, D), F32)], axis=0)
    h32, h16 = _ln_in(x_prompt.reshape(T, D), x_rest, ln_in_g, ln_in_b)
    z = _matmul(h16, w2t, F32, transposed_b=True, name="in_proj")

    cqn, kv32, kv16, kr32, kr16 = _post(z, cos_t, sin_t, q_norm_g[0], kv_norm_g[0], off)
    Q = _queries(cqn, w_uq3, cos_t, sin_t)
    K, V = _kv_expand(kv16, kr16, wuk_t, wuv_t)

    o_a_p = _prompt_attention(Q, K, V, B, SEQ, T, meta0)
    qs = jnp.transpose(_q_latent(Q, wuk_t, T // DB, DB), (1, 0, 2))
    acc = _paged_attention(page_table, qs, cache_kv[0], jnp.swapaxes(cache_kr[0], 1, 2),
                           kv16[T:M].reshape(DB, 1, KR), kr16[T:M].reshape(DB, 1, ROPE_DIM))
    o_a_s = _sample_values(jnp.transpose(acc, (1, 0, 2)), wuv_t)
    o_a = jnp.concatenate([o_a_p, o_a_s], axis=0)

    gates = z[:, off["gates"]:off["gates"] + 2 * H_M]

    def g_rows(L):
        return jnp.transpose(gates.reshape(R // L, L, 2 * H_M), (0, 2, 1)).reshape(R // L, 2 * H_M, 1, L)

    zc = jnp.zeros((B, H_M, dv, dqk), F32)
    zn = jnp.zeros((B, H_M, 1, dqk), F32)
    zm = jnp.zeros((B, H_M, 1, 1), F32)
    _, c1, n1, m1 = _mlstm_chunks(z, g_rows(N_META), b_if[0], m_norm_g[0], zc, zn, zm, off,
                                  B=B, L=N_META, nc=1, row0=meta0, out_rows=B * N_META, dqk=dqk, dv=dv)
    o_m_p, c_p, n_p, m_p = _mlstm_chunks(z, g_rows(CHUNK), b_if[0], m_norm_g[0], c1, n1, m1, off,
                                         B=B, L=CHUNK, nc=SEQ // CHUNK, row0=0, out_rows=T, dqk=dqk, dv=dv)
    zs = z[T:M]
    o_m_s, c_s, n_s, m_s = _mlstm_step(
        zs[:, off["mq"]:off["mq"] + H_M * dqk].reshape(DB, H_M, dqk),
        zs[:, off["mk"]:off["mk"] + H_M * dqk].reshape(DB, H_M, dqk),
        zs[:, off["mv"]:off["mv"] + H_M * dv].reshape(DB, H_M, dv),
        zs[:, off["mo"]:off["mo"] + H_M * dv].reshape(DB, H_M, dv),
        zs[:, off["gates"]:off["gates"] + 2 * H_M].reshape(DB, 2 * H_M, 1),
        b_if[0], m_norm_g[0], state_C[0], state_n[0], state_m[0].reshape(DB, H_M, 1))
    o_m = jnp.concatenate([o_m_p, o_m_s.reshape(DB, H_M * dv).astype(BF16)], axis=0)

    merged = _merge(o_a, o_m, w_br_a[0].astype(BF16), w_br_m[0].astype(BF16), z, off)
    mix = _matmul(merged, w_out[0].astype(BF16), F32, name="out_proj")
    hp32, eid, probs = _ln_router(h32, mix, ln1_g[0], ln1_b[0], w_router[0], b_router[0])

    n_super = (M * TOP_K) // MOE_SUPER + E
    slot_tok, slot_of, st_e, st_blk, st_nv, sub_valid = _moe_plan(eid[:, :TOP_K], E, n_super)
    xs = _dispatch(hp32, slot_tok, sub_valid, n_super * MOE_SUPER)
    act = _ffn_up(st_e, st_blk, st_nv, xs, w_gate[0], w_up[0], b_gate[0], b_up[0], n_super)
    ys = _ffn_down(st_e, st_blk, st_nv, act, w_down[0], b_down[0], n_super)
    tt = _tile(M, (64, 32, 16, 8))
    slots = jnp.transpose(slot_of.reshape(M // tt, tt, TOP_K), (0, 2, 1)).reshape(M // tt, 1, TOP_K * tt)
    y = _combine(slots, ys, probs, hp32, ln2_g[0], ln2_b[0])

    def with_meta(a):
        return jnp.concatenate([a[meta0:meta0 + B * N_META].reshape(B, N_META, -1),
                                a[:T].reshape(B, SEQ, -1)], axis=1)[None]

    return (y[:T].reshape(B, SEQ, D), y[T:M].reshape(DB, 1, D),
            with_meta(kv32), with_meta(kr32),
            c_p[None], n_p.reshape(1, B, H_M, dqk), m_p.reshape(1, B, H_M),
            kv32[T:M].reshape(1, DB, 1, KR), kr32[T:M].reshape(1, DB, 1, ROPE_DIM),
            c_s[None], n_s[None], m_s.reshape(1, DB, H_M))
```

```python
import functools

import jax
import jax.numpy as jnp
import numpy as np
from jax import lax
from jax.experimental import pallas as pl
from jax.experimental.pallas import tpu as pltpu

F32 = jnp.float32
BF16 = jnp.bfloat16

N_META = 16
H_A = 32
NOPE_DIM = 128
ROPE_DIM = 64
V_DIM = 128
ROPE_THETA = 10000.0
ATTN_SCALE = (NOPE_DIM + ROPE_DIM) ** -0.5
H_M = 8
CHUNK = 64
TOP_K = 4
SWIGLU_LIMIT = 7.0
SWIGLU_ALPHA = 1.702
DEPTH = 1
ALPHA = (2 * DEPTH) ** 0.25
EPS = 1e-5

LANES = 128
VMEM_LIMIT = 52 * 1024 * 1024
ROW_ALIGN = 256
MOE_SUPER = 1280
MOE_SUB = 256
MOE_NSUB = MOE_SUPER // MOE_SUB
PAGES_PER_STEP = 32
MLSTM_HEADS_PER_STEP = 4


def _cp(n_axes):
    return pltpu.CompilerParams(dimension_semantics=("arbitrary",) * n_axes, vmem_limit_bytes=VMEM_LIMIT)


def _tile(n, prefs):
    for t in prefs:
        if n % t == 0:
            return t
    return n


def _dot(a, b):
    return jnp.dot(a, b, preferred_element_type=F32)


def _dot_nt(a, b):
    return lax.dot_general(a, b, (((1,), (1,)), ((), ())), preferred_element_type=F32)


def _dot_tn(a, b):
    return lax.dot_general(a, b, (((0,), (0,)), ((), ())), preferred_element_type=F32)


def _log_sigmoid(x):
    return jnp.minimum(x, 0.0) - jnp.log(1.0 + jnp.exp(-jnp.abs(x)))


def _sigmoid(x):
    return 1.0 / (1.0 + jnp.exp(-x))


def _layernorm(x, g, b):
    mu = jnp.mean(x, axis=-1, keepdims=True)
    xc = x - mu
    var = jnp.mean(xc * xc, axis=-1, keepdims=True)
    return xc * lax.rsqrt(var + EPS) * g + b


def _rmsnorm(x, g):
    return x * lax.rsqrt(jnp.mean(x * x, axis=-1, keepdims=True) + EPS) * g


def _ln_in_kernel(xp_ref, xr_ref, g_ref, b_ref, o32_ref, o16_ref, *, n_prompt_blocks):
    def emit(x_ref):
        y = _layernorm(x_ref[...], g_ref[...], b_ref[...])
        o32_ref[...] = y
        o16_ref[...] = y.astype(BF16)

    @pl.when(pl.program_id(0) < n_prompt_blocks)
    def _():
        emit(xp_ref)

    @pl.when(pl.program_id(0) >= n_prompt_blocks)
    def _():
        emit(xr_ref)


def _ln_in(x_prompt, x_rest, g, b):
    T, D = x_prompt.shape
    R = T + x_rest.shape[0]
    tm = _tile(np.gcd(T, x_rest.shape[0]), (128, 64, 32, 16, 8))
    npb = T // tm
    row = pl.BlockSpec((tm, D), lambda i: (i, 0))
    vec = pl.BlockSpec((1, D), lambda i: (0, 0))
    return pl.pallas_call(
        functools.partial(_ln_in_kernel, n_prompt_blocks=npb), grid=(R // tm,),
        in_specs=[pl.BlockSpec((tm, D), lambda i: (jnp.minimum(i, npb - 1), 0)),
                  pl.BlockSpec((tm, D), lambda i: (jnp.maximum(i - npb, 0), 0)), vec, vec],
        out_specs=[row, row],
        out_shape=[jax.ShapeDtypeStruct((R, D), F32), jax.ShapeDtypeStruct((R, D), BF16)],
        compiler_params=_cp(1), name="ln_in")(x_prompt, x_rest, g.reshape(1, D), b.reshape(1, D))


def _mm_kernel(a_ref, b_ref, o_ref, *, transposed_b):
    dot = _dot_nt if transposed_b else _dot
    o_ref[...] = dot(a_ref[...], b_ref[...]).astype(o_ref.dtype)


def _matmul(a, b, out_dtype, transposed_b=False, name="matmul"):
    N, K = b.shape if transposed_b else b.shape[::-1]
    M = a.shape[0]
    tm = _tile(M, (384, 640, 256, 320, 128, 64, 32, 16))
    tn = _tile(N, (1024, 512, 256, 128))
    b_spec = (pl.BlockSpec((tn, K), lambda j, i: (j, 0)) if transposed_b
              else pl.BlockSpec((K, tn), lambda j, i: (0, j)))
    return pl.pallas_call(
        functools.partial(_mm_kernel, transposed_b=transposed_b), grid=(N // tn, M // tm),
        in_specs=[pl.BlockSpec((tm, K), lambda j, i: (i, 0)), b_spec],
        out_specs=pl.BlockSpec((tm, tn), lambda j, i: (i, j)),
        out_shape=jax.ShapeDtypeStruct((M, N), out_dtype),
        compiler_params=_cp(2), name=name)(a, b)


def _rope128(a, cos, sin):
    return a * cos + pltpu.roll(a, ROPE_DIM, 1) * sin


def _post_kernel(cq_ref, ckv_ref, kr_ref, cos_ref, sin_ref, qg_ref, kvg_ref,
                 cqn_ref, kv32_ref, kv16_ref, kr32_ref, kr16_ref):
    cqn_ref[...] = _rmsnorm(cq_ref[...], qg_ref[...]).astype(BF16)
    kvn = _rmsnorm(ckv_ref[...], kvg_ref[...])
    kv32_ref[...] = kvn
    kv16_ref[...] = kvn.astype(BF16)
    r = _rope128(kr_ref[...], cos_ref[...], sin_ref[...])[:, :ROPE_DIM]
    kr32_ref[...] = r
    kr16_ref[...] = r.astype(BF16)


def _post(z, cos, sin, q_norm_g, kv_norm_g, off):
    R = z.shape[0]
    QR, KR = q_norm_g.shape[0], kv_norm_g.shape[0]
    tm = _tile(R, (256, 128, 64, 32, 16))
    return pl.pallas_call(
        _post_kernel, grid=(R // tm,),
        in_specs=[pl.BlockSpec((tm, QR), lambda i: (i, off["cq"] // QR)),
                  pl.BlockSpec((tm, KR), lambda i: (i, off["ckv"] // KR)),
                  pl.BlockSpec((tm, LANES), lambda i: (i, off["kr"] // LANES)),
                  pl.BlockSpec((tm, LANES), lambda i: (i, 0)),
                  pl.BlockSpec((tm, LANES), lambda i: (i, 0)),
                  pl.BlockSpec((1, QR), lambda i: (0, 0)),
                  pl.BlockSpec((1, KR), lambda i: (0, 0))],
        out_specs=[pl.BlockSpec((tm, QR), lambda i: (i, 0)),
                   pl.BlockSpec((tm, KR), lambda i: (i, 0)),
                   pl.BlockSpec((tm, KR), lambda i: (i, 0)),
                   pl.BlockSpec((tm, ROPE_DIM), lambda i: (i, 0)),
                   pl.BlockSpec((tm, ROPE_DIM), lambda i: (i, 0))],
        out_shape=[jax.ShapeDtypeStruct((R, QR), BF16), jax.ShapeDtypeStruct((R, KR), F32),
                   jax.ShapeDtypeStruct((R, KR), BF16), jax.ShapeDtypeStruct((R, ROPE_DIM), F32),
                   jax.ShapeDtypeStruct((R, ROPE_DIM), BF16)],
        compiler_params=_cp(1), name="latent_norm_rope")(
            z, z, z, cos, sin, q_norm_g.reshape(1, QR), kv_norm_g.reshape(1, KR))


def _q_kernel(cqn_ref, w_ref, cos_ref, sin_ref, q_ref):
    acc = _dot(cqn_ref[...], w_ref[0])
    q_ref[0, :, 0:NOPE_DIM] = acc[:, :NOPE_DIM].astype(BF16)
    r = _rope128(acc[:, NOPE_DIM:], cos_ref[...], sin_ref[...])
    q_ref[0, :, NOPE_DIM:NOPE_DIM + ROPE_DIM] = r[:, :ROPE_DIM].astype(BF16)


def _queries(cqn, w_uq3, cos, sin):
    R, QR = cqn.shape
    H = w_uq3.shape[0]
    tm = _tile(R, (768, 512, 256, 128, 64, 32, 16))
    return pl.pallas_call(
        _q_kernel, grid=(R // tm, H),
        in_specs=[pl.BlockSpec((tm, QR), lambda i, h: (i, 0)),
                  pl.BlockSpec((1, QR, NOPE_DIM + 2 * ROPE_DIM), lambda i, h: (h, 0, 0)),
                  pl.BlockSpec((tm, LANES), lambda i, h: (i, 0)),
                  pl.BlockSpec((tm, LANES), lambda i, h: (i, 0))],
        out_specs=pl.BlockSpec((1, tm, NOPE_DIM + ROPE_DIM), lambda i, h: (h, i, 0)),
        out_shape=jax.ShapeDtypeStruct((H, R, NOPE_DIM + ROPE_DIM), BF16),
        compiler_params=_cp(2), name="mla_queries")(cqn, w_uq3, cos, sin)


def _kvx_kernel(kv16_ref, kr16_ref, wuk_ref, wuv_ref, k_ref, v_ref):
    kv = kv16_ref[...]
    k_ref[0, :, 0:NOPE_DIM] = _dot(kv, wuk_ref[0]).astype(BF16)
    k_ref[0, :, NOPE_DIM:NOPE_DIM + ROPE_DIM] = kr16_ref[...]
    v_ref[0] = _dot(kv, wuv_ref[0]).astype(BF16)


def _kv_expand(kv16, kr16, wuk_t, wuv_t):
    R, KR = kv16.shape
    H = wuk_t.shape[0]
    tm = _tile(R, (768, 512, 256, 128, 64, 32, 16))
    return pl.pallas_call(
        _kvx_kernel, grid=(R // tm, H),
        in_specs=[pl.BlockSpec((tm, KR), lambda i, h: (i, 0)),
                  pl.BlockSpec((tm, ROPE_DIM), lambda i, h: (i, 0)),
                  pl.BlockSpec((1, KR, NOPE_DIM), lambda i, h: (h, 0, 0)),
                  pl.BlockSpec((1, KR, V_DIM), lambda i, h: (h, 0, 0))],
        out_specs=[pl.BlockSpec((1, tm, NOPE_DIM + ROPE_DIM), lambda i, h: (h, i, 0)),
                   pl.BlockSpec((1, tm, V_DIM), lambda i, h: (h, i, 0))],
        out_shape=[jax.ShapeDtypeStruct((H, R, NOPE_DIM + ROPE_DIM), BF16),
                   jax.ShapeDtypeStruct((H, R, V_DIM), BF16)],
        compiler_params=_cp(2), name="mla_kv_expand")(kv16, kr16, wuk_t, wuv_t)


def _attn_kernel(q_ref, k_ref, v_ref, km_ref, vm_ref, o_ref, *, tq, tk):
    qi = pl.program_id(2)
    q = q_ref[0]
    s0 = _dot_nt(q, km_ref[0]) * ATTN_SCALE
    m = jnp.max(s0, axis=-1, keepdims=True)
    p0 = jnp.exp(s0 - m)
    l = jnp.sum(p0, axis=-1, keepdims=True)
    acc = _dot(p0.astype(BF16), vm_ref[0])

    def update(carry, s, v):
        m, l, acc = carry
        m_new = jnp.maximum(m, jnp.max(s, axis=-1, keepdims=True))
        p = jnp.exp(s - m_new)
        corr = jnp.exp(m - m_new)
        return m_new, l * corr + jnp.sum(p, axis=-1, keepdims=True), acc * corr + _dot(p.astype(BF16), v)

    def below_diagonal(j, carry):
        start = pl.multiple_of(j * tk, tk)
        s = _dot_nt(q, k_ref[0, pl.ds(start, tk), :]) * ATTN_SCALE
        return update(carry, s, v_ref[0, pl.ds(start, tk), :])

    carry = lax.fori_loop(0, qi * (tq // tk), below_diagonal, (m, l, acc))
    start = pl.multiple_of(qi * tq, tq)
    s = _dot_nt(q, k_ref[0, pl.ds(start, tq), :]) * ATTN_SCALE
    row = lax.broadcasted_iota(jnp.int32, (tq, tq), 0)
    col = lax.broadcasted_iota(jnp.int32, (tq, tq), 1)
    m, l, acc = update(carry, jnp.where(col <= row, s, -jnp.inf), v_ref[0, pl.ds(start, tq), :])
    o_ref[...] = (acc / l).astype(o_ref.dtype)


def _prompt_attention(Q, K, V, B, SEQ, out_rows, meta_row0):
    H = Q.shape[0]
    tq = _tile(SEQ, (512, 256, 128))
    tk = tq
    nq = SEQ // tq
    mb = meta_row0 // N_META
    dk = NOPE_DIM + ROPE_DIM
    return pl.pallas_call(
        functools.partial(_attn_kernel, tq=tq, tk=tk), grid=(B, H, nq),
        in_specs=[pl.BlockSpec((1, tq, dk), lambda b, h, i: (h, b * nq + i, 0)),
                  pl.BlockSpec((1, SEQ, dk), lambda b, h, i: (h, b, 0)),
                  pl.BlockSpec((1, SEQ, V_DIM), lambda b, h, i: (h, b, 0)),
                  pl.BlockSpec((1, N_META, dk), lambda b, h, i: (h, mb + b, 0)),
                  pl.BlockSpec((1, N_META, V_DIM), lambda b, h, i: (h, mb + b, 0))],
        out_specs=pl.BlockSpec((tq, V_DIM), lambda b, h, i: (b * nq + i, h)),
        out_shape=jax.ShapeDtypeStruct((out_rows, H * V_DIM), BF16),
        compiler_params=_cp(3), name="mla_prompt_attention")(Q, K, V, K, V)


def _qlat_kernel(q_ref, wuk_ref, o_ref):
    q = q_ref[0]
    kr = wuk_ref.shape[1]
    o_ref[0, :, 0:kr] = _dot_nt(q[:, :NOPE_DIM], wuk_ref[0]).astype(BF16)
    o_ref[0, :, kr:kr + ROPE_DIM] = q[:, NOPE_DIM:]


def _q_latent(Q, wuk_t, row_blk, DB):
    H, _, KR = wuk_t.shape[0], None, wuk_t.shape[1]
    dk = NOPE_DIM + ROPE_DIM
    return pl.pallas_call(
        _qlat_kernel, grid=(H,),
        in_specs=[pl.BlockSpec((1, DB, dk), lambda h: (h, row_blk, 0)),
                  pl.BlockSpec((1, KR, NOPE_DIM), lambda h: (h, 0, 0))],
        out_specs=pl.BlockSpec((1, DB, KR + ROPE_DIM), lambda h: (h, 0, 0)),
        out_shape=jax.ShapeDtypeStruct((H, DB, KR + ROPE_DIM), BF16),
        compiler_params=_cp(1), name="mla_q_latent")(Q, wuk_t)


def _paged_kernel(pt_ref, q_ref, kvs_ref, krs_ref, kv_hbm, krt_hbm, o_ref,
                  m_s, l_s, acc_s, kv_in, krt_in, kv_buf, krt_buf, sem, *, npg, kr):
    b = pl.program_id(0)
    j = pl.program_id(1)
    nj = pl.num_programs(1)
    step = b * nj + j
    slot = step % 2
    page = kv_in.shape[2]

    def page_copies(bb, jj, sl):
        copies = []
        for p in range(npg):
            pg = pt_ref[bb, jj * npg + p]
            copies.append(pltpu.make_async_copy(kv_hbm.at[pg], kv_in.at[sl, p], sem.at[sl, 0]))
            copies.append(pltpu.make_async_copy(krt_hbm.at[pg], krt_in.at[sl, p], sem.at[sl, 1]))
        return copies

    @pl.when(step == 0)
    def _():
        for c in page_copies(0, 0, 0):
            c.start()

    @pl.when(step + 1 < pl.num_programs(0) * nj)
    def _():
        wrap = j + 1 == nj
        for c in page_copies(jnp.where(wrap, b + 1, b), jnp.where(wrap, 0, j + 1), 1 - slot):
            c.start()

    for c in page_copies(b, j, slot):
        c.wait()

    @pl.when(j == 0)
    def _():
        m_s[...] = jnp.full(m_s.shape, -jnp.inf, F32)
        l_s[...] = jnp.zeros(l_s.shape, F32)
        acc_s[...] = jnp.zeros(acc_s.shape, F32)

    for p in range(npg):
        kv_buf[p * page:(p + 1) * page, :] = kv_in[slot, p].astype(BF16)
        krt_buf[:, p * page:(p + 1) * page] = krt_in[slot, p].astype(BF16)
    q = q_ref[0]
    ql = q[:, :kr]
    qr = q[:, kr:]
    kv16 = kv_buf[...]
    s = (_dot_nt(ql, kv16) + _dot(qr, krt_buf[...])) * ATTN_SCALE
    m_prev = m_s[...]
    m_new = jnp.maximum(m_prev, jnp.max(s, axis=-1, keepdims=True))
    corr = jnp.exp(m_prev - m_new)
    p = jnp.exp(s - m_new)
    l = l_s[...] * corr + jnp.sum(p, axis=-1, keepdims=True)
    acc = acc_s[...] * corr + _dot(p.astype(BF16), kv16)
    m_s[...] = m_new
    l_s[...] = l
    acc_s[...] = acc

    @pl.when(j == pl.num_programs(1) - 1)
    def _():
        kvs = kvs_ref[0].astype(F32)
        krs = krs_ref[0].astype(F32)
        s = (jnp.sum(ql.astype(F32) * kvs, axis=-1, keepdims=True)
             + jnp.sum(qr.astype(F32) * krs, axis=-1, keepdims=True)) * ATTN_SCALE
        m2 = jnp.maximum(m_new, s)
        p = jnp.exp(s - m2)
        c2 = jnp.exp(m_new - m2)
        o_ref[0] = ((acc * c2 + p * kvs) / (l * c2 + p)).astype(o_ref.dtype)


def _paged_attention(page_table, Qs, cache_kv, cache_krt, kvs, krs):
    DB, H, dq = Qs.shape
    _, PAGE, KR = cache_kv.shape
    NP = page_table.shape[1]
    npg = _tile(NP, (PAGES_PER_STEP, 16, 8, 4, 2, 1))

    in_specs = [pl.BlockSpec((1, H, dq), lambda b, j, pt: (b, 0, 0)),
                pl.BlockSpec((1, 1, KR), lambda b, j, pt: (b, 0, 0)),
                pl.BlockSpec((1, 1, ROPE_DIM), lambda b, j, pt: (b, 0, 0)),
                pl.BlockSpec(memory_space=pl.ANY), pl.BlockSpec(memory_space=pl.ANY)]
    return pl.pallas_call(
        functools.partial(_paged_kernel, npg=npg, kr=KR),
        grid_spec=pltpu.PrefetchScalarGridSpec(
            num_scalar_prefetch=1, grid=(DB, NP // npg), in_specs=in_specs,
            out_specs=pl.BlockSpec((1, H, KR), lambda b, j, pt: (b, 0, 0)),
            scratch_shapes=[pltpu.VMEM((H, 1), F32), pltpu.VMEM((H, 1), F32), pltpu.VMEM((H, KR), F32),
                            pltpu.VMEM((2, npg, PAGE, KR), F32), pltpu.VMEM((2, npg, ROPE_DIM, PAGE), F32),
                            pltpu.VMEM((npg * PAGE, KR), BF16), pltpu.VMEM((ROPE_DIM, npg * PAGE), BF16),
                            pltpu.SemaphoreType.DMA((2, 2))]),
        out_shape=jax.ShapeDtypeStruct((DB, H, KR), BF16),
        compiler_params=_cp(2), name="mla_paged_attention")(
            page_table, Qs, kvs, krs, cache_kv, cache_krt)


def _ov_kernel(a_ref, w_ref, o_ref):
    o_ref[...] = _dot(a_ref[0], w_ref[0]).astype(o_ref.dtype)


def _sample_values(acc_t, wuv_t):
    H, DB, KR = acc_t.shape
    return pl.pallas_call(
        _ov_kernel, grid=(H,),
        in_specs=[pl.BlockSpec((1, DB, KR), lambda h: (h, 0, 0)),
                  pl.BlockSpec((1, KR, V_DIM), lambda h: (h, 0, 0))],
        out_specs=pl.BlockSpec((DB, V_DIM), lambda h: (0, h)),
        out_shape=jax.ShapeDtypeStruct((DB, H * V_DIM), BF16),
        compiler_params=_cp(1), name="mla_sample_values")(acc_t, wuv_t)


def _mlstm_kernel(bif_ref, q_ref, k_ref, v_ref, og_ref, gc_ref, gi_ref, gf_ref, mng_ref, c0_ref, n0_ref, m0_ref,
                  om_ref, co_ref, no_ref, mo_ref, c_s, n_s, m_s, *, L, nc, dqk, dv, hps):
    c = pl.program_id(2)

    @pl.when(c == 0)
    def _():
        c_s[...] = c0_ref[0]
        n_s[...] = n0_ref[0]
        m_s[...] = m0_ref[0]

    gc = gc_ref[...]
    lane = lax.broadcasted_iota(jnp.int32, gc.shape, 1)
    row = lax.broadcasted_iota(jnp.int32, (L, L), 0)
    col = lax.broadcasted_iota(jnp.int32, (L, L), 1)
    tri = col <= row
    for hh in range(hps):
        h = pl.program_id(1) * hps + hh
        qk = slice(hh * dqk, (hh + 1) * dqk)
        vv = slice(hh * dv, (hh + 1) * dv)
        b_i = bif_ref[h]
        b_f = bif_ref[H_M + h]
        li_c = jnp.sum(jnp.where(lane == h, gc, 0.0), axis=1, keepdims=True) + b_i
        lf_c = _log_sigmoid(jnp.sum(jnp.where(lane == H_M + h, gc, 0.0), axis=1, keepdims=True) + b_f)
        li_r = gi_ref[0, hh] + b_i
        lf_r = _log_sigmoid(gf_ref[0, hh] + b_f)

        f_c = jnp.sum(jnp.where(tri, lf_r, 0.0), axis=1, keepdims=True)
        f_r = jnp.sum(jnp.where(row <= col, lf_c, 0.0), axis=0, keepdims=True)
        m_prev = m_s[hh]
        dmat = jnp.where(tri, f_c - f_r + li_r, -jnp.inf)
        b = f_c + m_prev
        m_row = jnp.maximum(b, jnp.max(dmat, axis=1, keepdims=True))

        qs = q_ref[:, qk] * (dqk ** -0.5)
        k = k_ref[:, qk]
        v = v_ref[:, vv]
        q16 = qs.astype(BF16)
        k16 = k.astype(BF16)
        w = _dot_nt(q16, k16) * jnp.exp(dmat - m_row)
        bw = jnp.exp(b - m_row)
        c_old = c_s[hh]
        n_old = n_s[hh]
        num = bw * _dot_nt(q16, c_old.astype(BF16)) + _dot(w.astype(BF16), v.astype(BF16))
        den = bw * jnp.sum(n_old * qs, axis=1, keepdims=True) + jnp.sum(w, axis=1, keepdims=True)
        hc = num / jnp.maximum(jnp.abs(den), jnp.exp(-m_row))

        m_new = m_row[L - 1:L, :]
        f_last = f_c[L - 1:L, :]
        w_s = jnp.exp(f_last - f_c + li_c - m_new)
        decay = jnp.exp(f_last + m_prev - m_new)
        c_s[hh] = decay * c_old + _dot_tn((v * w_s).astype(BF16), k16)
        n_s[hh] = decay * n_old + jnp.sum(w_s * k, axis=0, keepdims=True)
        m_s[hh] = m_new

        hn = hc * lax.rsqrt(jnp.mean(hc * hc, axis=-1, keepdims=True) + EPS) * mng_ref[:, vv]
        om_ref[:, vv] = (hn * _sigmoid(og_ref[:, vv])).astype(om_ref.dtype)

    @pl.when(c == nc - 1)
    def _():
        co_ref[0] = c_s[...]
        no_ref[0] = n_s[...]
        mo_ref[0] = m_s[...]


def _mlstm_chunks(z, g_rows, b_if, m_norm_g, c0, n0, m0, off, *, B, L, nc, row0, out_rows, dqk, dv):
    rb0 = row0 // L

    def rb(b, c):
        return rb0 + b * nc + c

    hps = MLSTM_HEADS_PER_STEP
    qw, vw = hps * dqk, hps * dv
    assert H_M % hps == 0 and all(off[n] % wd == 0 for n, wd in (("mq", qw), ("mk", qw), ("mv", vw), ("mo", vw)))
    state = lambda shp: pl.BlockSpec((1, hps) + shp, lambda b, h, c: (b, h, 0, 0))
    return pl.pallas_call(
        functools.partial(_mlstm_kernel, L=L, nc=nc, dqk=dqk, dv=dv, hps=hps), grid=(B, H_M // hps, nc),
        in_specs=[pl.BlockSpec(memory_space=pltpu.SMEM),
                  pl.BlockSpec((L, qw), lambda b, h, c: (rb(b, c), off["mq"] // qw + h)),
                  pl.BlockSpec((L, qw), lambda b, h, c: (rb(b, c), off["mk"] // qw + h)),
                  pl.BlockSpec((L, vw), lambda b, h, c: (rb(b, c), off["mv"] // vw + h)),
                  pl.BlockSpec((L, vw), lambda b, h, c: (rb(b, c), off["mo"] // vw + h)),
                  pl.BlockSpec((L, LANES), lambda b, h, c: (rb(b, c), off["gates"] // LANES)),
                  pl.BlockSpec((1, hps, 1, L), lambda b, h, c: (rb(b, c), h, 0, 0)),
                  pl.BlockSpec((1, hps, 1, L), lambda b, h, c: (rb(b, c), H_M // hps + h, 0, 0)),
                  pl.BlockSpec((1, vw), lambda b, h, c: (0, h)),
                  state((dv, dqk)), state((1, dqk)), state((1, 1))],
        out_specs=[pl.BlockSpec((L, vw), lambda b, h, c: (b * nc + c, h)),
                   state((dv, dqk)), state((1, dqk)), state((1, 1))],
        out_shape=[jax.ShapeDtypeStruct((out_rows, H_M * dv), BF16),
                   jax.ShapeDtypeStruct((B, H_M, dv, dqk), F32),
                   jax.ShapeDtypeStruct((B, H_M, 1, dqk), F32),
                   jax.ShapeDtypeStruct((B, H_M, 1, 1), F32)],
        scratch_shapes=[pltpu.VMEM((hps, dv, dqk), F32), pltpu.VMEM((hps, 1, dqk), F32),
                        pltpu.VMEM((hps, 1, 1), F32)],
        compiler_params=_cp(3), name=f"mlstm_chunks_L{L}")(
            b_if, z, z, z, z, z, g_rows, g_rows, m_norm_g.reshape(1, H_M * dv), c0, n0, m0)


def _mlstm_step_kernel(q_ref, k_ref, v_ref, og_ref, g_ref, bif_ref, mng_ref, c_ref, n_ref, m_ref,
                       om_ref, co_ref, no_ref, mo_ref, *, dqk):
    g = g_ref[0] + bif_ref[...]
    li = g[0:H_M]
    lf = _log_sigmoid(g[H_M:2 * H_M])
    m_prev = m_ref[0]
    qs = q_ref[0] * (dqk ** -0.5)
    k = k_ref[0]
    v = v_ref[0]
    n_old = n_ref[0]
    b = lf + m_prev
    m_row = jnp.maximum(b, li)
    q16 = qs.astype(BF16)
    k16 = k.astype(BF16)
    v16 = v.astype(BF16)
    w = jnp.sum(q16.astype(F32) * k16.astype(F32), axis=1, keepdims=True) * jnp.exp(li - m_row)
    bw = jnp.exp(b - m_row)
    den = bw * jnp.sum(n_old * qs, axis=1, keepdims=True) + w
    dn = jnp.maximum(jnp.abs(den), jnp.exp(-m_row))
    w_s = jnp.exp(li - m_row)
    decay = jnp.exp(lf + m_prev - m_row)
    no_ref[0] = decay * n_old + w_s * k
    mo_ref[0] = m_row
    vw = v * w_s
    zeros_v = jnp.zeros_like(vw)
    k_pad = jnp.concatenate([k, jnp.zeros_like(k)], axis=0).astype(BF16)
    hrow = lax.broadcasted_iota(jnp.int32, vw.shape, 0)
    outs = []
    for h in range(H_M):
        c_old = c_ref[0, h]
        cq = _dot_nt(q16, c_old.astype(BF16))[h:h + 1]
        num = bw[h:h + 1] * cq + w[h:h + 1] * v16[h:h + 1].astype(F32)
        outs.append(num / dn[h:h + 1])
        vw_h = jnp.concatenate([jnp.where(hrow == h, vw, 0.0), zeros_v], axis=0).astype(BF16)
        co_ref[0, h] = decay[h:h + 1] * c_old + _dot_tn(vw_h, k_pad)
    hc = jnp.concatenate(outs, axis=0)
    hn = hc * lax.rsqrt(jnp.mean(hc * hc, axis=-1, keepdims=True) + EPS) * mng_ref[...]
    om_ref[0] = hn * _sigmoid(og_ref[0])


def _mlstm_step(q, k, v, og, g, b_if, m_norm_g, c, n, m):
    DB, _, dqk = q.shape
    dv = v.shape[2]
    per_b = lambda *shp: pl.BlockSpec((1,) + shp, lambda b: (b,) + (0,) * len(shp))
    full = lambda *shp: pl.BlockSpec(shp, lambda b: (0,) * len(shp))
    return pl.pallas_call(
        functools.partial(_mlstm_step_kernel, dqk=dqk), grid=(DB,),
        in_specs=[per_b(H_M, dqk), per_b(H_M, dqk), per_b(H_M, dv), per_b(H_M, dv), per_b(2 * H_M, 1),
                  full(2 * H_M, 1), full(H_M, dv), per_b(H_M, dv, dqk), per_b(H_M, dqk), per_b(H_M, 1)],
        out_specs=[per_b(H_M, dv), per_b(H_M, dv, dqk), per_b(H_M, dqk), per_b(H_M, 1)],
        out_shape=[jax.ShapeDtypeStruct((DB, H_M, dv), F32), jax.ShapeDtypeStruct((DB, H_M, dv, dqk), F32),
                   jax.ShapeDtypeStruct((DB, H_M, dqk), F32), jax.ShapeDtypeStruct((DB, H_M, 1), F32)],
        compiler_params=_cp(1), name="mlstm_step")(
            q, k, v, og, g, b_if.reshape(2 * H_M, 1), m_norm_g.reshape(H_M, dv), c, n, m)


def _merge_kernel(oa_ref, om_ref, wa_ref, wm_ref, ga_ref, gm_ref, o_ref):
    a = _dot(oa_ref[...], wa_ref[...])
    m = _dot(om_ref[...], wm_ref[...])
    o_ref[...] = (_sigmoid(ga_ref[...]) * a + _sigmoid(gm_ref[...]) * m).astype(o_ref.dtype)


def _merge(o_a, o_m, w_br_a, w_br_m, z, off):
    M, Ka = o_a.shape
    Km = o_m.shape[1]
    N = w_br_a.shape[1]
    tm = _tile(M, (640, 320, 256, 128, 64, 16))
    tn = _tile(N, (512, 256, 128))
    return pl.pallas_call(
        _merge_kernel, grid=(M // tm, N // tn),
        in_specs=[pl.BlockSpec((tm, Ka), lambda i, j: (i, 0)),
                  pl.BlockSpec((tm, Km), lambda i, j: (i, 0)),
                  pl.BlockSpec((Ka, tn), lambda i, j: (0, j)),
                  pl.BlockSpec((Km, tn), lambda i, j: (0, j)),
                  pl.BlockSpec((tm, tn), lambda i, j: (i, off["ga"] // tn + j)),
                  pl.BlockSpec((tm, tn), lambda i, j: (i, off["gm"] // tn + j))],
        out_specs=pl.BlockSpec((tm, tn), lambda i, j: (i, j)),
        out_shape=jax.ShapeDtypeStruct((M, N), BF16),
        compiler_params=_cp(2), name="branch_merge")(o_a, o_m, w_br_a, w_br_m, z, z)


SLAB = 8


def _store_slabs(o_ref, y):
    w = o_ref.shape[2]
    for i in range(SLAB):
        o_ref[:, i, :] = y[:, i * w:(i + 1) * w]


def _load_slabs(x_ref):
    return jnp.concatenate([x_ref[:, i, :] for i in range(SLAB)], axis=1)


def _ln_router_kernel(h_ref, mix_ref, g_ref, b_ref, wr_ref, br_ref, o32_ref, ei_ref, pr_ref, *, n_exp):
    y = _layernorm(ALPHA * h_ref[...] + mix_ref[...], g_ref[...], b_ref[...])
    _store_slabs(o32_ref, y)
    logits = _dot(y.astype(BF16), wr_ref[...]) + br_ref[...]
    lane = lax.broadcasted_iota(jnp.int32, logits.shape, 1)
    logits = jnp.where(lane < n_exp, logits, -jnp.inf)
    vals, idxs = [], []
    for _ in range(TOP_K):
        mx = jnp.max(logits, axis=1, keepdims=True)
        ix = jnp.min(jnp.where(logits == mx, lane, LANES), axis=1, keepdims=True)
        vals.append(mx)
        idxs.append(ix)
        logits = jnp.where(lane == ix, -jnp.inf, logits)
    es = [jnp.exp(vv - vals[0]) for vv in vals]
    tot = es[0]
    for e in es[1:]:
        tot = tot + e
    ei = jnp.zeros(lane.shape, jnp.int32)
    pr = jnp.zeros(lane.shape, F32)
    for kk in range(TOP_K):
        ei = jnp.where(lane == kk, idxs[kk], ei)
        pr = jnp.where(lane == kk, es[kk] / tot, pr)
    ei_ref[...] = ei
    pr_ref[...] = pr


def _ln_router(h32, mix, g, b, w_router, b_router):
    M, D = mix.shape
    E = w_router.shape[1]
    wr = jnp.zeros((D, LANES), BF16).at[:, :E].set(w_router.astype(BF16))
    br = jnp.zeros((1, LANES), F32).at[0, :E].set(b_router)
    tm = _tile(M, (128, 64, 32, 16, 8))
    row = pl.BlockSpec((tm, D), lambda i: (i, 0))
    vec = pl.BlockSpec((1, D), lambda i: (0, 0))
    nar = pl.BlockSpec((tm, LANES), lambda i: (i, 0))
    return pl.pallas_call(
        functools.partial(_ln_router_kernel, n_exp=E), grid=(M // tm,),
        in_specs=[row, row, vec, vec, pl.BlockSpec((D, LANES), lambda i: (0, 0)),
                  pl.BlockSpec((1, LANES), lambda i: (0, 0))],
        out_specs=[pl.BlockSpec((tm, SLAB, D // SLAB), lambda i: (i, 0, 0)), nar, nar],
        out_shape=[jax.ShapeDtypeStruct((M, SLAB, D // SLAB), F32),
                   jax.ShapeDtypeStruct((M, LANES), jnp.int32), jax.ShapeDtypeStruct((M, LANES), F32)],
        compiler_params=_cp(1), name="ln1_router")(h32, mix, g.reshape(1, D), b.reshape(1, D), wr, br)


def _gather_kernel(valid_ref, tok_ref, x_ref, o_ref, buf, sem):
    i = pl.program_id(0)

    @pl.when(valid_ref[i] > 0)
    def _():
        def copy(r):
            return pltpu.make_async_copy(x_ref.at[tok_ref[0, 0, r]], buf.at[r], sem)

        def start(r, _):
            copy(r).start()
            return 0

        def wait(r, _):
            copy(r).wait()
            return 0

        lax.fori_loop(0, MOE_SUB, start, 0, unroll=8)
        lax.fori_loop(0, MOE_SUB, wait, 0, unroll=8)
        o_ref[...] = _load_slabs(buf).astype(o_ref.dtype)

    @pl.when(valid_ref[i] == 0)
    def _():
        o_ref[...] = jnp.zeros(o_ref.shape, o_ref.dtype)


def _dispatch(x3, slot_tok, sub_valid, n_slots):
    D = x3.shape[1] * x3.shape[2]
    nsub = n_slots // MOE_SUB
    return pl.pallas_call(
        _gather_kernel,
        grid_spec=pltpu.PrefetchScalarGridSpec(
            num_scalar_prefetch=1, grid=(nsub,),
            in_specs=[pl.BlockSpec((1, 1, MOE_SUB), lambda i, v: (i, 0, 0), memory_space=pltpu.SMEM),
                      pl.BlockSpec(memory_space=pl.ANY)],
            out_specs=pl.BlockSpec((MOE_SUB, D), lambda i, v: (i, 0)),
            scratch_shapes=[pltpu.VMEM((MOE_SUB,) + x3.shape[1:], F32), pltpu.SemaphoreType.DMA(())]),
        out_shape=jax.ShapeDtypeStruct((n_slots, D), BF16),
        compiler_params=_cp(1), name="moe_dispatch")(
            sub_valid, slot_tok.reshape(nsub, 1, MOE_SUB), x3)


def _row_variants():
    half = MOE_SUB // 2
    return sorted({MOE_SUB, MOE_SUPER - 2 * half, MOE_SUPER - half, MOE_SUPER})


def _per_super_tile(used, o_ref, compute):
    lo = 0
    for rows in _row_variants():
        @pl.when(jnp.logical_and(used > lo, used <= rows))
        def _(rows=rows):
            compute(rows)
            if rows < MOE_SUPER:
                o_ref[rows:, :] = jnp.zeros((MOE_SUPER - rows, o_ref.shape[1]), o_ref.dtype)
        lo = rows

    @pl.when(used == 0)
    def _():
        o_ref[...] = jnp.zeros(o_ref.shape, o_ref.dtype)


def _ffn_up_kernel(se_ref, sb_ref, nv_ref, x_ref, wg_ref, wu_ref, bg_ref, bu_ref, o_ref):
    def compute(rows):
        x = x_ref[0:rows, :]
        g = jnp.minimum(_dot(x, wg_ref[0].astype(BF16)) + bg_ref[0], SWIGLU_LIMIT)
        u = jnp.clip(_dot(x, wu_ref[0].astype(BF16)) + bu_ref[0], -SWIGLU_LIMIT, SWIGLU_LIMIT)
        o_ref[0:rows, :] = (g * _sigmoid(SWIGLU_ALPHA * g) * (u + 1.0)).astype(o_ref.dtype)

    _per_super_tile(nv_ref[pl.program_id(0)], o_ref, compute)


def _weight_chunk(n_chunks):
    def f(s, c, se, sb, nv):
        return (se[s], 0, jnp.where(nv[s] > 0, c, n_chunks - 1))
    return f


def _ffn_up(st_e, st_blk, st_nv, xs, w_gate, w_up, b_gate, b_up, n_super):
    E, D, F = w_gate.shape
    tf = _tile(F, (256, 128))
    w_spec = pl.BlockSpec((1, D, tf), _weight_chunk(F // tf))
    b_spec = pl.BlockSpec((1, 1, tf), _weight_chunk(F // tf))
    return pl.pallas_call(
        _ffn_up_kernel,
        grid_spec=pltpu.PrefetchScalarGridSpec(
            num_scalar_prefetch=3, grid=(n_super, F // tf),
            in_specs=[pl.BlockSpec((MOE_SUPER, D), lambda s, c, se, sb, nv: (sb[s], 0)),
                      w_spec, w_spec, b_spec, b_spec],
            out_specs=pl.BlockSpec((MOE_SUPER, tf), lambda s, c, se, sb, nv: (s, c))),
        out_shape=jax.ShapeDtypeStruct((n_super * MOE_SUPER, F), BF16),
        compiler_params=_cp(2), name="moe_ffn_up")(
            st_e, st_blk, st_nv, xs, w_gate, w_up, b_gate.reshape(E, 1, F), b_up.reshape(E, 1, F))


def _ffn_down_kernel(se_ref, sb_ref, nv_ref, a_ref, wd_ref, bd_ref, o_ref):
    def compute(rows):
        o_ref[0:rows, :] = _dot(a_ref[0:rows, :], wd_ref[0].astype(BF16)) + bd_ref[0]

    _per_super_tile(nv_ref[pl.program_id(0)], o_ref, compute)


def _ffn_down(st_e, st_blk, st_nv, act, w_down, b_down, n_super):
    E, F, D = w_down.shape
    tn = _tile(D, (256, 128))
    return pl.pallas_call(
        _ffn_down_kernel,
        grid_spec=pltpu.PrefetchScalarGridSpec(
            num_scalar_prefetch=3, grid=(n_super, D // tn),
            in_specs=[pl.BlockSpec((MOE_SUPER, F), lambda s, c, se, sb, nv: (sb[s], 0)),
                      pl.BlockSpec((1, F, tn), _weight_chunk(D // tn)),
                      pl.BlockSpec((1, 1, tn), _weight_chunk(D // tn))],
            out_specs=pl.BlockSpec((MOE_SUPER, tn), lambda s, c, se, sb, nv: (s, c))),
        out_shape=jax.ShapeDtypeStruct((n_super * MOE_SUPER, D), F32),
        compiler_params=_cp(2), name="moe_ffn_down")(
            st_e, st_blk, st_nv, act, w_down, b_down.reshape(E, 1, D))


def _combine_kernel(slot_ref, ys_ref, pr_ref, h_ref, g_ref, b_ref, o_ref, buf, sem, *, tt):
    def copy(r):
        k = r // tt
        t = r - k * tt
        return pltpu.make_async_copy(ys_ref.at[pl.ds(slot_ref[0, 0, r], 1), :],
                                     buf.at[k, pl.ds(t, 1), :], sem)

    def start(r, _):
        copy(r).start()
        return 0

    def wait(r, _):
        copy(r).wait()
        return 0

    lax.fori_loop(0, TOP_K * tt, start, 0, unroll=8)
    lax.fori_loop(0, TOP_K * tt, wait, 0, unroll=8)
    pr = pr_ref[...]
    y = pr[:, 0:1] * buf[0]
    for k in range(1, TOP_K):
        y = y + pr[:, k:k + 1] * buf[k]
    o_ref[...] = _layernorm(ALPHA * _load_slabs(h_ref) + y, g_ref[...], b_ref[...])


def _combine(slots, ys, probs, h3, g, b):
    M = h3.shape[0]
    D = h3.shape[1] * h3.shape[2]
    tt = _tile(M, (64, 32, 16, 8))
    row = pl.BlockSpec((tt, D), lambda i: (i, 0))
    vec = pl.BlockSpec((1, D), lambda i: (0, 0))
    return pl.pallas_call(
        functools.partial(_combine_kernel, tt=tt), grid=(M // tt,),
        in_specs=[pl.BlockSpec((1, 1, TOP_K * tt), lambda i: (i, 0, 0), memory_space=pltpu.SMEM),
                  pl.BlockSpec(memory_space=pl.ANY),
                  pl.BlockSpec((tt, LANES), lambda i: (i, 0)),
                  pl.BlockSpec((tt,) + h3.shape[1:], lambda i: (i, 0, 0)), vec, vec],
        out_specs=row,
        out_shape=jax.ShapeDtypeStruct((M, D), F32),
        scratch_shapes=[pltpu.VMEM((TOP_K, tt, D), F32), pltpu.SemaphoreType.DMA(())],
        compiler_params=_cp(1), name="moe_combine_ln2")(
            slots, ys, probs, h3, g.reshape(1, D), b.reshape(1, D))


def _moe_plan(eid, n_exp, n_super):
    M = eid.shape[0]
    S = M * TOP_K
    ef = eid.reshape(S)
    onehot = (ef[:, None] == jnp.arange(n_exp, dtype=jnp.int32)[None, :]).astype(jnp.int32)
    seen = jnp.cumsum(onehot, axis=0)
    cnt = seen[-1]
    rank = jnp.sum((seen - onehot) * onehot, axis=1)
    nsup = (cnt + MOE_SUPER - 1) // MOE_SUPER
    sup_end = jnp.cumsum(nsup)
    sup_start = sup_end - nsup
    slot_of = sup_start[ef] * MOE_SUPER + rank
    n_slots = n_super * MOE_SUPER
    slot_tok = jnp.zeros((n_slots,), jnp.int32).at[slot_of].set(jnp.arange(S, dtype=jnp.int32) // TOP_K)
    total = sup_end[-1]
    sidx = jnp.arange(n_super, dtype=jnp.int32)
    used = sidx < total
    s_eff = jnp.minimum(sidx, total - 1)
    st_e = jnp.minimum(jnp.searchsorted(sup_end, s_eff, side="right"), n_exp - 1).astype(jnp.int32)
    rows = jnp.where(used, jnp.clip(cnt[st_e] - (s_eff - sup_start[st_e]) * MOE_SUPER, 0, MOE_SUPER), 0)
    sub_valid = (rows[:, None] > jnp.arange(MOE_NSUB, dtype=jnp.int32)[None, :] * MOE_SUB)
    st_nv = rows.astype(jnp.int32)
    return (slot_tok, slot_of, st_e, s_eff.astype(jnp.int32), st_nv,
            sub_valid.astype(jnp.int32).reshape(n_super * MOE_NSUB))


def kernel(x_prompt, x_sample, cache_kv, cache_kr, state_C, state_n, state_m, page_table, meta, ln_in_g, ln_in_b,
           w_in, b_if, q_norm_g, w_uq, kv_norm_g, w_uk, w_uv, m_norm_g, w_br_a, w_br_m, w_out, ln1_g, ln1_b,
           w_router, b_router, w_gate, b_gate, w_up, b_up, w_down, b_down, ln2_g, ln2_b):
    B, SEQ, D = x_prompt.shape
    DB, DS, _ = x_sample.shape
    assert DS == 1 and w_in.shape[0] == DEPTH == 1
    NP = page_table.shape[1]
    PAGE = cache_kv.shape[2]
    QR, KR = q_norm_g.shape[1], kv_norm_g.shape[1]
    E = w_router.shape[2]
    dqk, dv = D // 2 // H_M, D // H_M
    T = B * SEQ
    M = T + DB
    meta0 = M
    R = -(-(M + B * N_META) // ROW_ALIGN) * ROW_ALIGN
    assert T % DB == 0 and M % N_META == 0 and SEQ % CHUNK == 0

    wt = jnp.swapaxes(w_in[0], 0, 1)
    sizes = (QR, KR, ROPE_DIM, H_M * dqk, H_M * dqk, H_M * dv, H_M * dv, H_M, H_M, D, D)
    starts = np.concatenate([[0], np.cumsum(sizes)]).tolist()
    seg = {n: wt[starts[i]:starts[i + 1]] for i, n in enumerate(
        ("cq", "ckv", "kr", "mq", "mk", "mv", "mo", "mi", "mf", "ga", "gm"))}

    def half_swap(wr, axis):
        lo, hi = jnp.split(wr, 2, axis=axis)
        return jnp.concatenate([-hi, lo], axis=axis)

    groups = [("mq", seg["mq"]), ("mk", seg["mk"]), ("mv", seg["mv"]), ("mo", seg["mo"]), ("ga", seg["ga"]),
              ("gm", seg["gm"]), ("cq", seg["cq"]), ("ckv", seg["ckv"]),
              ("kr", jnp.concatenate([seg["kr"], half_swap(seg["kr"], 0)], axis=0)),
              ("gates", jnp.concatenate([seg["mi"], seg["mf"], jnp.zeros((LANES - 2 * H_M, D), F32)], axis=0))]
    off, pos = {}, 0
    for n, g in groups:
        off[n] = pos
        pos += g.shape[0]
    n2 = -(-pos // 1024) * 1024
    w2t = jnp.concatenate([g for _, g in groups] + [jnp.zeros((n2 - pos, D), F32)], axis=0).astype(BF16)
    for n, blk in (("mq", dqk), ("mk", dqk), ("mv", dv), ("mo", dv), ("cq", QR), ("ckv", KR), ("kr", LANES),
                   ("gates", LANES), ("ga", 512), ("gm", 512)):
        assert off[n] % blk == 0, (n, off[n], blk)

    dk = NOPE_DIM + ROPE_DIM
    wq = w_uq[0].reshape(QR, H_A, dk)
    wq_r = wq[..., NOPE_DIM:]
    w_uq3 = jnp.transpose(jnp.concatenate([wq, half_swap(wq_r, 2)], axis=-1), (1, 0, 2)).astype(BF16)
    wuk_t = jnp.transpose(w_uk[0], (1, 0, 2)).astype(BF16)
    wuv_t = jnp.transpose(w_uv[0], (1, 0, 2)).astype(BF16)

    pos_rows = np.zeros((R,), np.float32)
    pos_rows[:T] = np.tile(N_META + np.arange(SEQ), B)
    pos_rows[T:M] = NP * PAGE
    pos_rows[meta0:meta0 + B * N_META] = np.tile(np.arange(N_META), B)
    inv = 1.0 / (ROPE_THETA ** (jnp.arange(0, ROPE_DIM, 2, dtype=F32) / ROPE_DIM))
    ang = jnp.asarray(pos_rows)[:, None] * inv
    cos_t = jnp.tile(jnp.cos(ang), (1, LANES // (ROPE_DIM // 2)))
    sin_t = jnp.tile(jnp.sin(ang), (1, LANES // (ROPE_DIM // 2)))

    x_rest = jnp.concatenate([x_sample.reshape(DB, D), jnp.tile(meta.astype(F32), (B, 1)),
                              jnp.zeros((R - M - B * N_META, D), F32)], axis=0)
    h32, h16 = _ln_in(x_prompt.reshape(T, D), x_rest, ln_in_g, ln_in_b)
    z = _matmul(h16, w2t, F32, transposed_b=True, name="in_proj")

    cqn, kv32, kv16, kr32, kr16 = _post(z, cos_t, sin_t, q_norm_g[0], kv_norm_g[0], off)
    Q = _queries(cqn, w_uq3, cos_t, sin_t)
    K, V = _kv_expand(kv16, kr16, wuk_t, wuv_t)

    o_a_p = _prompt_attention(Q, K, V, B, SEQ, T, meta0)
    qs = jnp.transpose(_q_latent(Q, wuk_t, T // DB, DB), (1, 0, 2))
    acc = _paged_attention(page_table, qs, cache_kv[0], jnp.swapaxes(cache_kr[0], 1, 2),
                           kv16[T:M].reshape(DB, 1, KR), kr16[T:M].reshape(DB, 1, ROPE_DIM))
    o_a_s = _sample_values(jnp.transpose(acc, (1, 0, 2)), wuv_t)
    o_a = jnp.concatenate([o_a_p, o_a_s], axis=0)

    gates = z[:, off["gates"]:off["gates"] + 2 * H_M]

    def g_rows(L):
        return jnp.transpose(gates.reshape(R // L, L, 2 * H_M), (0, 2, 1)).reshape(R // L, 2 * H_M, 1, L)

    zc = jnp.zeros((B, H_M, dv, dqk), F32)
    zn = jnp.zeros((B, H_M, 1, dqk), F32)
    zm = jnp.zeros((B, H_M, 1, 1), F32)
    _, c1, n1, m1 = _mlstm_chunks(z, g_rows(N_META), b_if[0], m_norm_g[0], zc, zn, zm, off,
                                  B=B, L=N_META, nc=1, row0=meta0, out_rows=B * N_META, dqk=dqk, dv=dv)
    o_m_p, c_p, n_p, m_p = _mlstm_chunks(z, g_rows(CHUNK), b_if[0], m_norm_g[0], c1, n1, m1, off,
                                         B=B, L=CHUNK, nc=SEQ // CHUNK, row0=0, out_rows=T, dqk=dqk, dv=dv)
    zs = z[T:M]
    o_m_s, c_s, n_s, m_s = _mlstm_step(
        zs[:, off["mq"]:off["mq"] + H_M * dqk].reshape(DB, H_M, dqk),
        zs[:, off["mk"]:off["mk"] + H_M * dqk].reshape(DB, H_M, dqk),
        zs[:, off["mv"]:off["mv"] + H_M * dv].reshape(DB, H_M, dv),
        zs[:, off["mo"]:off["mo"] + H_M * dv].reshape(DB, H_M, dv),
        zs[:, off["gates"]:off["gates"] + 2 * H_M].reshape(DB, 2 * H_M, 1),
        b_if[0], m_norm_g[0], state_C[0], state_n[0], state_m[0].reshape(DB, H_M, 1))
    o_m = jnp.concatenate([o_m_p, o_m_s.reshape(DB, H_M * dv).astype(BF16)], axis=0)

    merged = _merge(o_a, o_m, w_br_a[0].astype(BF16), w_br_m[0].astype(BF16), z, off)
    mix = _matmul(merged, w_out[0].astype(BF16), F32, name="out_proj")
    hp32, eid, probs = _ln_router(h32, mix, ln1_g[0], ln1_b[0], w_router[0], b_router[0])

    n_super = (M * TOP_K) // MOE_SUPER + E
    slot_tok, slot_of, st_e, st_blk, st_nv, sub_valid = _moe_plan(eid[:, :TOP_K], E, n_super)
    xs = _dispatch(hp32, slot_tok, sub_valid, n_super * MOE_SUPER)
    act = _ffn_up(st_e, st_blk, st_nv, xs, w_gate[0], w_up[0], b_gate[0], b_up[0], n_super)
    ys = _ffn_down(st_e, st_blk, st_nv, act, w_down[0], b_down[0], n_super)
    tt = _tile(M, (64, 32, 16, 8))
    slots = jnp.transpose(slot_of.reshape(M // tt, tt, TOP_K), (0, 2, 1)).reshape(M // tt, 1, TOP_K * tt)
    y = _combine(slots, ys, probs, hp32, ln2_g[0], ln2_b[0])

    def with_meta(a):
        return jnp.concatenate([a[meta0:meta0 + B * N_META].reshape(B, N_META, -1),
                                a[:T].reshape(B, SEQ, -1)], axis=1)[None]

    return (y[:T].reshape(B, SEQ, D), y[T:M].reshape(DB, 1, D),
            with_meta(kv32), with_meta(kr32),
            c_p[None], n_p.reshape(1, B, H_M, dqk), m_p.reshape(1, B, H_M),
            kv32[T:M].reshape(1, DB, 1, KR), kr32[T:M].reshape(1, DB, 1, ROPE_DIM),
            c_s[None], n_s[None], m_s.reshape(1, DB, H_M))
```
